```python
import jax, jax.numpy as jnp
from jax import lax
import numpy as np

D_MODEL = 2048
BATCH = 2
SEQ = 4096
DEPTH = 4
DEC_BATCH = 32
DEC_SEQ = 4
PAST_LEN = 16384
PAGE_SIZE = 128

HEAD_DIM = 64
N_HEADS = D_MODEL // HEAD_DIM
N_KV_HEADS = N_HEADS // 8
GROUP = N_HEADS // N_KV_HEADS
WINDOW = 128
BLOCK = 128
ROPE_THETA = 10000.0
Q_DIM = N_HEADS * HEAD_DIM
KV_DIM = N_KV_HEADS * HEAD_DIM
POOL_WINDOWS = (2, 4, 8, 16)
N_POOL_GROUPS = len(POOL_WINDOWS)
POOL_GROUP_DIM = D_MODEL // 8
POOL_DIM = N_POOL_GROUPS * POOL_GROUP_DIM
POOL_OUT_GROUP = D_MODEL // N_POOL_GROUPS
POOL_HIST = max(POOL_WINDOWS) - 1
IN_DIM = Q_DIM + 2 * KV_DIM + POOL_DIM + 2 * D_MODEL
SPLITS = (Q_DIM, Q_DIM + KV_DIM, Q_DIM + 2 * KV_DIM, Q_DIM + 2 * KV_DIM + POOL_DIM,
          Q_DIM + 2 * KV_DIM + POOL_DIM + D_MODEL)
D_FF = 5632
N_EXPERTS = 8
TOP_K = 2
D_FF_EXPERT = D_MODEL
N_DENSE = (DEPTH + 1) // 2
N_MOE = DEPTH // 2
EPS = 1e-5

kernel_name = 'hybrid_swa_sink_pool_moe_step'


def rms_norm(x, g):
    xf = x.astype(jnp.float32)
    y = xf * lax.rsqrt(jnp.mean(xf * xf, axis=-1, keepdims=True) + EPS)
    return (y * g.astype(jnp.float32)).astype(x.dtype)


def rope(x, pos):
    half = HEAD_DIM // 2
    inv = ROPE_THETA ** (-2.0 * jnp.arange(half, dtype=jnp.float32) / HEAD_DIM)
    ang = pos.astype(jnp.float32)[:, None] * inv[None, :]
    cos = jnp.cos(ang)[None, :, None, :]
    sin = jnp.sin(ang)[None, :, None, :]
    xf = x.astype(jnp.float32)
    x1, x2 = xf[..., :half], xf[..., half:]
    return jnp.concatenate([x1 * cos - x2 * sin, x2 * cos + x1 * sin], axis=-1).astype(x.dtype)


def sink_attention(q, k, v, mask, sinks):
    s = jnp.einsum('bnqkgd,bnskd->bnkgqs', q, k, preferred_element_type=jnp.float32) * (HEAD_DIM ** -0.5)
    s = jnp.where(mask[None, :, None, None], s, -jnp.inf)
    sink = sinks.astype(jnp.float32).reshape(N_KV_HEADS, GROUP)[None, None, :, :, None, None]
    m = jnp.maximum(jnp.max(s, axis=-1, keepdims=True), sink)
    p = jnp.exp(s - m)
    denom = jnp.sum(p, axis=-1, keepdims=True) + jnp.exp(sink - m)
    p = (p / denom).astype(v.dtype)
    return jnp.einsum('bnkgqs,bnskd->bnqkgd', p, v)


def attn_prompt(q, k, v, sinks):
    B, S = q.shape[0], q.shape[1]
    nb = S // BLOCK
    qb = q.reshape(B, nb, BLOCK, N_KV_HEADS, GROUP, HEAD_DIM)

    def band(t):
        tb = t.reshape(B, nb, BLOCK, N_KV_HEADS, HEAD_DIM)
        prev = jnp.concatenate([jnp.zeros_like(tb[:, :1]), tb[:, :-1]], axis=1)
        return jnp.concatenate([prev, tb], axis=2)

    i = jnp.arange(BLOCK)[:, None]
    j = jnp.arange(2 * BLOCK)[None, :]
    dist = i + BLOCK - j
    kpos = jnp.arange(nb)[:, None, None] * BLOCK - BLOCK + j[None]
    mask = (dist >= 0) & (dist <= WINDOW) & (kpos >= 0)
    o = sink_attention(qb, band(k), band(v), mask, sinks)
    return o.reshape(B, S, Q_DIM)


def attn_sample(q, k, v, k_cache, v_cache, sinks):
    DB, T = q.shape[0], q.shape[1]
    wc = k_cache.shape[1]
    k_all = jnp.concatenate([k_cache, k], axis=1)
    v_all = jnp.concatenate([v_cache, v], axis=1)
    qpos = PAST_LEN + jnp.arange(T)
    kpos = PAST_LEN - wc + jnp.arange(wc + T)
    dist = qpos[:, None] - kpos[None, :]
    mask = ((dist >= 0) & (dist <= WINDOW))[None]
    o = sink_attention(q.reshape(DB, 1, T, N_KV_HEADS, GROUP, HEAD_DIM),
                       k_all[:, None], v_all[:, None], mask, sinks)
    return o.reshape(DB, T, Q_DIM), k_all[:, -wc:], v_all[:, -wc:]


def pool_mix(u_ext, pos0, T):
    L = u_ext.shape[1]
    uf = u_ext.astype(jnp.float32)
    cs = jnp.cumsum(uf, axis=1)
    pos = pos0 + jnp.arange(L)
    outs = []
    for gi, w in enumerate(POOL_WINDOWS):
        sl = slice(gi * POOL_GROUP_DIM, (gi + 1) * POOL_GROUP_DIM)
        c = cs[..., sl]
        shifted = jnp.pad(c[:, :-w], ((0, 0), (w, 0), (0, 0)))
        cnt = jnp.minimum(pos + 1, w).astype(jnp.float32)[None, :, None]
        outs.append((c - shifted) / cnt - uf[..., sl])
    y = jnp.concatenate(outs, axis=-1)[:, L - T:]
    return y.astype(u_ext.dtype)


def token_mixer(xn, pos, w_in, sinks, w_pool_map, pool_scale, w_out, cache):
    Bn, T = xn.shape[0], xn.shape[1]
    z = xn @ w_in
    q, k, v, u, g_a, g_p = jnp.split(z, SPLITS, axis=-1)
    q = rope(q.reshape(Bn, T, N_HEADS, HEAD_DIM), pos)
    k = rope(k.reshape(Bn, T, N_KV_HEADS, HEAD_DIM), pos)
    v = v.reshape(Bn, T, N_KV_HEADS, HEAD_DIM)
    if cache is None:
        a = attn_prompt(q, k, v, sinks)
        k_new, v_new = k[:, -WINDOW:], v[:, -WINDOW:]
        p = pool_mix(u, 0, T)
        pool_new = u[:, -POOL_HIST:]
    else:
        k_cache, v_cache, pool_buf = cache
        a, k_new, v_new = attn_sample(q, k, v, k_cache, v_cache, sinks)
        u_ext = jnp.concatenate([pool_buf, u], axis=1)
        p = pool_mix(u_ext, PAST_LEN - pool_buf.shape[1], T)
        pool_new = u_ext[:, -POOL_HIST:]
    p = jnp.einsum('btgc,gco->btgo', p.reshape(Bn, T, N_POOL_GROUPS, POOL_GROUP_DIM), w_pool_map)
    p = p.reshape(Bn, T, D_MODEL) * pool_scale
    mix = jax.nn.sigmoid(g_a) * a + jax.nn.sigmoid(g_p) * p
    return mix @ w_out, k_new, v_new, pool_new


def swiglu(x, wg, wu, wd):
    return (jax.nn.silu(x @ wg) * (x @ wu)) @ wd


def moe_swiglu(x, w_router, b_router, wg, wu, wd):
    shp = x.shape
    xt = x.reshape(-1, D_MODEL)
    logits = (xt @ w_router).astype(jnp.float32) + b_router.astype(jnp.float32)
    top_v, top_i = lax.top_k(logits, TOP_K)
    gate = jax.nn.softmax(top_v, axis=-1)
    comb = jnp.sum(jax.nn.one_hot(top_i, N_EXPERTS, dtype=jnp.float32) * gate[..., None], axis=1)
    h = jax.nn.silu(jnp.einsum('nd,edf->nef', xt, wg)) * jnp.einsum('nd,edf->nef', xt, wu)
    h = h * comb.astype(h.dtype)[..., None]
    y = jnp.einsum('nef,efd->nd', h, wd)
    return y.reshape(shp)


def trunk(x, pos, k_cache, v_cache, pool_state, norm_mix, norm_ffn, norm_final, w_in, attn_sinks,
          w_pool_map, pool_scale, w_out, w_ffn_gate, w_ffn_up, w_ffn_down, w_router, b_router,
          w_exp_gate, w_exp_up, w_exp_down):
    ks, vs, ps = [], [], []
    for l in range(DEPTH):
        cache = None if k_cache is None else (k_cache[l], v_cache[l], pool_state[l])
        xn = rms_norm(x, norm_mix[l])
        m, kn, vn, pn = token_mixer(xn, pos, w_in[l], attn_sinks[l], w_pool_map[l], pool_scale[l], w_out[l], cache)
        x = x + m
        ks.append(kn); vs.append(vn); ps.append(pn)
        xn = rms_norm(x, norm_ffn[l])
        if l % 2 == 0:
            i = l // 2
            x = x + swiglu(xn, w_ffn_gate[i], w_ffn_up[i], w_ffn_down[i])
        else:
            i = l // 2
            x = x + moe_swiglu(xn, w_router[i], b_router[i], w_exp_gate[i], w_exp_up[i], w_exp_down[i])
    return rms_norm(x, norm_final), jnp.stack(ks), jnp.stack(vs), jnp.stack(ps)


def setup_inputs(seed: int = 0) -> dict:
    key = jax.random.key(seed)
    ks = jax.random.split(key, 24)

    def nrm(k, shape, scale=1.0):
        return jax.random.normal(k, shape, jnp.float32) * scale

    return {
        'x_prompt': nrm(ks[0], (BATCH, SEQ, D_MODEL)),
        'x_sample': nrm(ks[1], (DEC_BATCH, DEC_SEQ, D_MODEL)),
        'cache_k': nrm(ks[2], (DEPTH, DEC_BATCH, WINDOW, N_KV_HEADS, HEAD_DIM)),
        'cache_v': nrm(ks[3], (DEPTH, DEC_BATCH, WINDOW, N_KV_HEADS, HEAD_DIM)),
        'state_pool': nrm(ks[4], (DEPTH, DEC_BATCH, POOL_HIST, POOL_DIM)),
        'norm_mix': 1.0 + nrm(ks[5], (DEPTH, D_MODEL), 0.02),
        'norm_ffn': 1.0 + nrm(ks[6], (DEPTH, D_MODEL), 0.02),
        'norm_final': 1.0 + nrm(ks[7], (D_MODEL,), 0.02),
        'w_in': nrm(ks[8], (DEPTH, D_MODEL, IN_DIM), D_MODEL ** -0.5),
        'attn_sinks': nrm(ks[9], (DEPTH, N_HEADS), 0.5),
        'w_pool_map': nrm(ks[10], (DEPTH, N_POOL_GROUPS, POOL_GROUP_DIM, POOL_OUT_GROUP), POOL_GROUP_DIM ** -0.5),
        'pool_scale': 1.0 + nrm(ks[11], (DEPTH, D_MODEL), 0.1),
        'w_out': nrm(ks[12], (DEPTH, D_MODEL, D_MODEL), D_MODEL ** -0.5),
        'w_ffn_gate': nrm(ks[13], (N_DENSE, D_MODEL, D_FF), D_MODEL ** -0.5),
        'w_ffn_up': nrm(ks[14], (N_DENSE, D_MODEL, D_FF), D_MODEL ** -0.5),
        'w_ffn_down': nrm(ks[15], (N_DENSE, D_FF, D_MODEL), D_FF ** -0.5),
        'w_router': nrm(ks[16], (N_MOE, D_MODEL, N_EXPERTS), D_MODEL ** -0.5),
        'b_router': nrm(ks[17], (N_MOE, N_EXPERTS), 0.01),
        'w_exp_gate': nrm(ks[18], (N_MOE, N_EXPERTS, D_MODEL, D_FF_EXPERT), D_MODEL ** -0.5),
        'w_exp_up': nrm(ks[19], (N_MOE, N_EXPERTS, D_MODEL, D_FF_EXPERT), D_MODEL ** -0.5),
        'w_exp_down': nrm(ks[20], (N_MOE, N_EXPERTS, D_FF_EXPERT, D_MODEL), D_FF_EXPERT ** -0.5),
    }


def reference(x_prompt, x_sample, cache_k, cache_v, state_pool, norm_mix, norm_ffn, norm_final, w_in,
              attn_sinks, w_pool_map, pool_scale, w_out, w_ffn_gate, w_ffn_up, w_ffn_down, w_router,
              b_router, w_exp_gate, w_exp_up, w_exp_down):
    pos_prompt = jnp.arange(x_prompt.shape[1], dtype=jnp.int32)
    pos_sample = PAST_LEN + jnp.arange(x_sample.shape[1], dtype=jnp.int32)
    y_prompt, k_p, v_p, pool_p = trunk(
        x_prompt, pos_prompt, None, None, None, norm_mix, norm_ffn, norm_final, w_in, attn_sinks,
        w_pool_map, pool_scale, w_out, w_ffn_gate, w_ffn_up, w_ffn_down, w_router, b_router,
        w_exp_gate, w_exp_up, w_exp_down)
    y_sample, k_s, v_s, pool_s = trunk(
        x_sample, pos_sample, cache_k, cache_v, state_pool, norm_mix, norm_ffn, norm_final, w_in,
        attn_sinks, w_pool_map, pool_scale, w_out, w_ffn_gate, w_ffn_up, w_ffn_down, w_router,
        b_router, w_exp_gate, w_exp_up, w_exp_down)
    return (y_prompt, y_sample, k_p, v_p, pool_p, k_s, v_s, pool_s)
```

```python
import functools

import jax
import jax.numpy as jnp
from jax import lax
from jax.experimental import pallas as pl
from jax.experimental.pallas import tpu as pltpu

D_MODEL = 2048
HEAD_DIM = 64
N_HEADS = 32
N_KV_HEADS = 4
GROUP = N_HEADS // N_KV_HEADS
KV_DIM = N_KV_HEADS * HEAD_DIM
WINDOW = 128
BLOCK = 128
ROPE_THETA = 10000.0
PAST_LEN = 16384
POOL_WINDOWS = (2, 4, 8, 16)
POOL_GROUP_DIM = 256
POOL_DIM = 1024
POOL_OUT_GROUP = 512
POOL_HIST = 15
N_EXPERTS = 8
EPS = 1e-5
NEG = -1e30

BF16 = jnp.bfloat16
F32 = jnp.float32

LANES = 128
VMEM_LIMIT = 52 * 1024 * 1024

IN_TN = 512
IN_Q_TILES = D_MODEL // IN_TN
IN_KV_TILE = IN_Q_TILES
IN_U_START = IN_KV_TILE + 1
IN_U_TILES = POOL_DIM // IN_TN
IN_GA_START = IN_U_START + IN_U_TILES
IN_G_TILES = D_MODEL // IN_TN
IN_GP_START = IN_GA_START + IN_G_TILES
IN_TILES = IN_GP_START + IN_G_TILES


def _sigmoid(x):
    return 1.0 / (1.0 + jnp.exp(-x))


def _split(a):
    hi = a.astype(BF16)
    lo = (a - hi.astype(F32)).astype(BF16)
    return hi, lo


def _dot(a, b):
    return jnp.dot(a, b, preferred_element_type=F32)


def _dot_nt(a, b):
    return lax.dot_general(a, b, (((1,), (1,)), ((), ())), preferred_element_type=F32)


def _mm(a_hi, a_lo, w):
    if w.dtype == BF16:
        return _dot(a_hi, w)
    w_hi, w_lo = _split(w)
    return _dot(a_hi, w_hi) + (_dot(a_lo, w_hi) + _dot(a_hi, w_lo))


def _qk(q, k_hi, k_lo):
    if q.dtype == BF16:
        return _dot_nt(q, k_hi)
    q_hi, q_lo = _split(q)
    return _dot_nt(q_hi, k_hi) + (_dot_nt(q_lo, k_hi) + _dot_nt(q_hi, k_lo))


def _pv(p, v_hi, v_lo):
    if v_lo is None:
        return _dot(p.astype(BF16), v_hi)
    p_hi, p_lo = _split(p)
    return _dot(p_hi, v_hi) + (_dot(p_lo, v_hi) + _dot(p_hi, v_lo))


def _kv_operands(t, precise):
    return _split(t) if precise else (t.astype(BF16), None)


def _cols(t, c):
    return None if t is None else t[:, c]


def _pool_map(pm, w_ref, gi):
    pm_hi, pm_lo = _split(pm) if w_ref.dtype == F32 else (pm.astype(BF16), None)
    return _mm(pm_hi, pm_lo, w_ref[gi])


def _rms_norm(x, g):
    ms = jnp.mean(x * x, axis=-1, keepdims=True)
    return x * lax.rsqrt(ms + EPS) * g


def _rope(t, cos, sin_signed):
    width = t.shape[1]
    lane = lax.broadcasted_iota(jnp.int32, t.shape, 1)
    first_half = (lane % HEAD_DIM) < (HEAD_DIM // 2)
    rot = jnp.where(first_half, pltpu.roll(t, width - HEAD_DIM // 2, 1), pltpu.roll(t, HEAD_DIM // 2, 1))
    reps = width // cos.shape[1]
    return t * jnp.tile(cos, (1, reps)) + rot * jnp.tile(sin_signed, (1, reps))


def _inproj_kernel(x_ref, g_ref, w_ref, cos_ref, sin_ref, q_ref, k_ref, v_ref, u_ref, ga_ref, gp_ref,
                   xhi_sc, xlo_sc):
    j = pl.program_id(1)
    precise = w_ref.dtype == F32

    @pl.when(j == 0)
    def _():
        xn = _rms_norm(x_ref[...], g_ref[...])
        if precise:
            xhi_sc[...], xlo_sc[...] = _split(xn)
        else:
            xhi_sc[...] = xn.astype(BF16)

    z = _mm(xhi_sc[...], xlo_sc[...] if precise else None, w_ref[...])

    @pl.when(j < IN_Q_TILES)
    def _():
        q_ref[...] = (_rope(z, cos_ref[...], sin_ref[...]) * (HEAD_DIM ** -0.5)).astype(q_ref.dtype)

    @pl.when(j == IN_KV_TILE)
    def _():
        k_ref[...] = _rope(z[:, :KV_DIM], cos_ref[...], sin_ref[...])
        v_ref[...] = z[:, KV_DIM:]

    @pl.when((j >= IN_U_START) & (j < IN_GA_START))
    def _():
        u_ref[...] = z

    @pl.when((j >= IN_GA_START) & (j < IN_GP_START))
    def _():
        ga_ref[...] = z

    @pl.when(j >= IN_GP_START)
    def _():
        gp_ref[...] = z


def _inproj(x, g, w, cos, sin, *, tm):
    m = x.shape[0]
    pos_blocks = cos.shape[0] // tm
    grid = (m // tm, IN_TILES)

    def clip(j, lo, n):
        return jnp.clip(j - lo, 0, n - 1)

    lo_rows = tm if w.dtype == F32 else 16
    return pl.pallas_call(
        _inproj_kernel,
        grid=grid,
        in_specs=[
            pl.BlockSpec((tm, D_MODEL), lambda i, j: (i, 0)),
            pl.BlockSpec((1, D_MODEL), lambda i, j: (0, 0)),
            pl.BlockSpec((D_MODEL, IN_TN), lambda i, j: (0, j)),
            pl.BlockSpec((tm, LANES), lambda i, j: (i % pos_blocks, 0)),
            pl.BlockSpec((tm, LANES), lambda i, j: (i % pos_blocks, 0)),
        ],
        out_specs=[
            pl.BlockSpec((tm, IN_TN), lambda i, j: (i, clip(j, 0, IN_Q_TILES))),
            pl.BlockSpec((tm, KV_DIM), lambda i, j: (i, 0)),
            pl.BlockSpec((tm, KV_DIM), lambda i, j: (i, 0)),
            pl.BlockSpec((tm, IN_TN), lambda i, j: (i, clip(j, IN_U_START, IN_U_TILES))),
            pl.BlockSpec((tm, IN_TN), lambda i, j: (i, clip(j, IN_GA_START, IN_G_TILES))),
            pl.BlockSpec((tm, IN_TN), lambda i, j: (i, clip(j, IN_GP_START, IN_G_TILES))),
        ],
        out_shape=[
            jax.ShapeDtypeStruct((m, D_MODEL), w.dtype),
            jax.ShapeDtypeStruct((m, KV_DIM), F32),
            jax.ShapeDtypeStruct((m, KV_DIM), F32),
            jax.ShapeDtypeStruct((m, POOL_DIM), F32),
            jax.ShapeDtypeStruct((m, D_MODEL), F32),
            jax.ShapeDtypeStruct((m, D_MODEL), F32),
        ],
        scratch_shapes=[pltpu.VMEM((tm, D_MODEL), BF16), pltpu.VMEM((lo_rows, D_MODEL), BF16)],
        compiler_params=pltpu.CompilerParams(
            dimension_semantics=("parallel", "arbitrary"), vmem_limit_bytes=VMEM_LIMIT),
        name="inproj_split" if w.dtype == F32 else "inproj",
    )(x, g, w, cos, sin)


def _mixer_prompt_kernel(sinks_ref, q_ref, kc_ref, kp_ref, vc_ref, vp_ref, uc_ref, up_ref, ga_ref, gp_ref,
                         wpool_ref, pscale_ref, *rest, first_block):
    precise = q_ref.dtype == F32
    if precise:
        mix_ref, ext_sc = rest
    else:
        x_ref, wout_ref, o_ref, ext_sc, mix_ref = rest
    n = first_block + pl.program_id(1)
    not_first = n > 0

    k_hi, k_lo = _kv_operands(jnp.concatenate([kp_ref[...], kc_ref[...]], axis=0), precise)
    v_hi, v_lo = _kv_operands(jnp.concatenate([vp_ref[...], vc_ref[...]], axis=0), precise)
    qi = lax.broadcasted_iota(jnp.int32, (BLOCK, 2 * BLOCK), 0)
    kj = lax.broadcasted_iota(jnp.int32, (BLOCK, 2 * BLOCK), 1)
    dist = qi + BLOCK - kj
    mask = (dist >= 0) & (dist <= WINDOW) & ((kj >= BLOCK) | not_first)

    ext_sc[0:16, :] = jnp.where(not_first, up_ref[BLOCK - 16:BLOCK, :], 0.0)
    ext_sc[16:16 + BLOCK, :] = uc_ref[...]
    pos = n * BLOCK + lax.broadcasted_iota(jnp.int32, (BLOCK, 1), 0)

    for kh in range(N_KV_HEADS):
        w = POOL_WINDOWS[kh]
        pcs = slice(kh * POOL_GROUP_DIM, (kh + 1) * POOL_GROUP_DIM)
        cur = ext_sc[16:16 + BLOCK, pcs]
        acc = cur
        for d in range(1, w):
            acc = acc + ext_sc[16 - d:16 - d + BLOCK, pcs]
        cnt = jnp.minimum(pos + 1, w).astype(F32)
        pm = acc / cnt - cur
        ocs = slice(kh * POOL_OUT_GROUP, (kh + 1) * POOL_OUT_GROUP)
        pg = _pool_map(pm, wpool_ref, kh) * pscale_ref[:, ocs]

        hs = slice(kh * HEAD_DIM, (kh + 1) * HEAD_DIM)
        parts = []
        for g in range(GROUP):
            h = kh * GROUP + g
            s = _qk(q_ref[:, h * HEAD_DIM:(h + 1) * HEAD_DIM], k_hi[:, hs], _cols(k_lo, hs))
            s = jnp.where(mask, s, NEG)
            sink = sinks_ref[h]
            m = jnp.maximum(jnp.max(s, axis=-1, keepdims=True), sink)
            p = jnp.exp(s - m)
            denom = jnp.sum(p, axis=-1, keepdims=True) + jnp.exp(sink - m)
            parts.append(_pv(p, v_hi[:, hs], _cols(v_lo, hs)) / denom)
        a = jnp.concatenate(parts, axis=1)
        mix = _sigmoid(ga_ref[:, ocs]) * a + _sigmoid(gp_ref[:, ocs]) * pg
        mix_ref[:, ocs] = mix.astype(mix_ref.dtype)

    if not precise:
        o_ref[...] = x_ref[...] + _dot(mix_ref[...], wout_ref[...])


def _mixer_prompt(q, k, v, u, ga, gp, x, sinks, wpool, pscale, wout, *, batch, seq):
    nb = seq // BLOCK
    m = batch * seq

    def cur(b, n):
        return (b * nb + n, 0)

    def prev(b, n):
        return (jnp.maximum(b * nb + n - 1, 0), 0)

    return pl.pallas_call(
        functools.partial(_mixer_prompt_kernel, first_block=0),
        grid=(batch, nb),
        in_specs=[
            pl.BlockSpec(memory_space=pltpu.SMEM),
            pl.BlockSpec((BLOCK, D_MODEL), cur),
            pl.BlockSpec((BLOCK, KV_DIM), cur),
            pl.BlockSpec((BLOCK, KV_DIM), prev),
            pl.BlockSpec((BLOCK, KV_DIM), cur),
            pl.BlockSpec((BLOCK, KV_DIM), prev),
            pl.BlockSpec((BLOCK, POOL_DIM), cur),
            pl.BlockSpec((BLOCK, POOL_DIM), prev),
            pl.BlockSpec((BLOCK, D_MODEL), cur),
            pl.BlockSpec((BLOCK, D_MODEL), cur),
            pl.BlockSpec((N_KV_HEADS, POOL_GROUP_DIM, POOL_OUT_GROUP), lambda b, n: (0, 0, 0)),
            pl.BlockSpec((1, D_MODEL), lambda b, n: (0, 0)),
            pl.BlockSpec((BLOCK, D_MODEL), cur),
            pl.BlockSpec((D_MODEL, D_MODEL), lambda b, n: (0, 0)),
        ],
        out_specs=pl.BlockSpec((BLOCK, D_MODEL), cur),
        out_shape=jax.ShapeDtypeStruct((m, D_MODEL), F32),
        scratch_shapes=[pltpu.VMEM((16 + BLOCK, POOL_DIM), F32), pltpu.VMEM((BLOCK, D_MODEL), BF16)],
        compiler_params=pltpu.CompilerParams(
            dimension_semantics=("parallel", "arbitrary"), vmem_limit_bytes=VMEM_LIMIT),
        name="mixer_prompt",
    )(sinks, q, k, k, v, v, u, u, ga, gp, wpool, pscale, x, wout)


def _mixer_prompt_tail(q_t, k, v, u, ga_t, gp_t, sinks, wpool, pscale, *, batch, seq):
    nb = seq // BLOCK

    def tail(b, n):
        return (b, 0)

    def cur(b, n):
        return (b * nb + nb - 1, 0)

    def prev(b, n):
        return (jnp.maximum(b * nb + nb - 2, 0), 0)

    return pl.pallas_call(
        functools.partial(_mixer_prompt_kernel, first_block=nb - 1),
        grid=(batch, 1),
        in_specs=[
            pl.BlockSpec(memory_space=pltpu.SMEM),
            pl.BlockSpec((BLOCK, D_MODEL), tail),
            pl.BlockSpec((BLOCK, KV_DIM), cur),
            pl.BlockSpec((BLOCK, KV_DIM), prev),
            pl.BlockSpec((BLOCK, KV_DIM), cur),
            pl.BlockSpec((BLOCK, KV_DIM), prev),
            pl.BlockSpec((BLOCK, POOL_DIM), cur),
            pl.BlockSpec((BLOCK, POOL_DIM), prev),
            pl.BlockSpec((BLOCK, D_MODEL), tail),
            pl.BlockSpec((BLOCK, D_MODEL), tail),
            pl.BlockSpec((N_KV_HEADS, POOL_GROUP_DIM, POOL_OUT_GROUP), lambda b, n: (0, 0, 0)),
            pl.BlockSpec((1, D_MODEL), lambda b, n: (0, 0)),
        ],
        out_specs=pl.BlockSpec((BLOCK, D_MODEL), tail),
        out_shape=jax.ShapeDtypeStruct((batch * BLOCK, D_MODEL), F32),
        scratch_shapes=[pltpu.VMEM((16 + BLOCK, POOL_DIM), F32)],
        compiler_params=pltpu.CompilerParams(
            dimension_semantics=("parallel", "arbitrary"), vmem_limit_bytes=VMEM_LIMIT),
        name="mixer_prompt_tail",
    )(sinks, q_t, k, k, v, v, u, u, ga_t, gp_t, wpool, pscale)


SAMPLE_DB = 8


def _mixer_sample_kernel(sinks_ref, q_ref, kn_ref, vn_ref, u_ref, ga_ref, gp_ref, ck_ref, cv_ref, pool_ref,
                         wpool_ref, pscale_ref, mix_ref, nk_ref, nv_ref, npool_ref, ext_sc, pm_sc, a_sc,
                         *, dec_seq):
    t_new = dec_seq
    rows = SAMPLE_DB * t_new
    precise = q_ref.dtype == F32
    kc_hi, kc_lo = _kv_operands(ck_ref[...].reshape(SAMPLE_DB * WINDOW, KV_DIM), precise)
    vc_hi, vc_lo = _kv_operands(cv_ref[...].reshape(SAMPLE_DB * WINDOW, KV_DIM), precise)
    kn_hi, kn_lo = _kv_operands(kn_ref[...], precise)
    vn_hi, vn_lo = _kv_operands(vn_ref[...], precise)

    rho = lax.broadcasted_iota(jnp.int32, (GROUP * rows, 1), 0)
    q_db = (rho % rows) // t_new
    q_t = rho % t_new
    q_g = rho // rows
    colc = lax.broadcasted_iota(jnp.int32, (1, SAMPLE_DB * WINDOW), 1)
    mask_c = (q_db == colc // WINDOW) & ((colc % WINDOW) >= q_t)
    coln = lax.broadcasted_iota(jnp.int32, (1, rows), 1)
    mask_n = (q_db == coln // t_new) & ((coln % t_new) <= q_t)

    for d in range(SAMPLE_DB):
        rs = slice(d * t_new, (d + 1) * t_new)
        ext_sc[d, 0:POOL_HIST, :] = pool_ref[d]
        ext_sc[d, POOL_HIST:POOL_HIST + t_new, :] = u_ref[rs, :]
        npool_ref[d] = ext_sc[d, t_new:t_new + POOL_HIST, :]
        nk_ref[d, 0:WINDOW - t_new, :] = ck_ref[d, t_new:WINDOW, :]
        nk_ref[d, WINDOW - t_new:WINDOW, :] = kn_ref[rs, :]
        nv_ref[d, 0:WINDOW - t_new, :] = cv_ref[d, t_new:WINDOW, :]
        nv_ref[d, WINDOW - t_new:WINDOW, :] = vn_ref[rs, :]
        for gi, w in enumerate(POOL_WINDOWS):
            pcs = slice(gi * POOL_GROUP_DIM, (gi + 1) * POOL_GROUP_DIM)
            cur = ext_sc[d, POOL_HIST:POOL_HIST + t_new, pcs]
            acc = cur
            for j in range(1, w):
                acc = acc + ext_sc[d, POOL_HIST - j:POOL_HIST - j + t_new, pcs]
            pm_sc[rs, pcs] = acc / float(w) - cur

    for kh in range(N_KV_HEADS):
        hs = slice(kh * HEAD_DIM, (kh + 1) * HEAD_DIM)
        qs = jnp.concatenate(
            [q_ref[:, (kh * GROUP + g) * HEAD_DIM:(kh * GROUP + g + 1) * HEAD_DIM] for g in range(GROUP)], axis=0)
        sc = jnp.where(mask_c, _qk(qs, kc_hi[:, hs], _cols(kc_lo, hs)), NEG)
        sn = jnp.where(mask_n, _qk(qs, kn_hi[:, hs], _cols(kn_lo, hs)), NEG)
        sink = jnp.zeros((GROUP * rows, 1), F32)
        for g in range(GROUP):
            sink = jnp.where(q_g == g, sinks_ref[kh * GROUP + g], sink)
        m = jnp.maximum(jnp.maximum(jnp.max(sc, axis=-1, keepdims=True), jnp.max(sn, axis=-1, keepdims=True)), sink)
        pc = jnp.exp(sc - m)
        pn = jnp.exp(sn - m)
        denom = jnp.sum(pc, axis=-1, keepdims=True) + jnp.sum(pn, axis=-1, keepdims=True) + jnp.exp(sink - m)
        o = (_pv(pc, vc_hi[:, hs], _cols(vc_lo, hs)) + _pv(pn, vn_hi[:, hs], _cols(vn_lo, hs))) / denom
        for g in range(GROUP):
            h = kh * GROUP + g
            a_sc[:, h * HEAD_DIM:(h + 1) * HEAD_DIM] = o[g * rows:(g + 1) * rows, :]

    for gi in range(N_KV_HEADS):
        pcs = slice(gi * POOL_GROUP_DIM, (gi + 1) * POOL_GROUP_DIM)
        ocs = slice(gi * POOL_OUT_GROUP, (gi + 1) * POOL_OUT_GROUP)
        pg = _pool_map(pm_sc[:, pcs], wpool_ref, gi) * pscale_ref[:, ocs]
        mix_ref[:, ocs] = _sigmoid(ga_ref[:, ocs]) * a_sc[:, ocs] + _sigmoid(gp_ref[:, ocs]) * pg


def _mixer_sample(q, kn, vn, u, ga, gp, ck, cv, pool, sinks, wpool, pscale, *, dec_seq):
    db = ck.shape[0]
    rows = SAMPLE_DB * dec_seq
    m = db * dec_seq
    row_blk = lambda i: (i, 0)
    db_blk = lambda i: (i, 0, 0)
    return pl.pallas_call(
        functools.partial(_mixer_sample_kernel, dec_seq=dec_seq),
        grid=(db // SAMPLE_DB,),
        in_specs=[
            pl.BlockSpec(memory_space=pltpu.SMEM),
            pl.BlockSpec((rows, D_MODEL), row_blk),
            pl.BlockSpec((rows, KV_DIM), row_blk),
            pl.BlockSpec((rows, KV_DIM), row_blk),
            pl.BlockSpec((rows, POOL_DIM), row_blk),
            pl.BlockSpec((rows, D_MODEL), row_blk),
            pl.BlockSpec((rows, D_MODEL), row_blk),
            pl.BlockSpec((SAMPLE_DB, WINDOW, KV_DIM), db_blk),
            pl.BlockSpec((SAMPLE_DB, WINDOW, KV_DIM), db_blk),
            pl.BlockSpec((SAMPLE_DB, POOL_HIST, POOL_DIM), db_blk),
            pl.BlockSpec((N_KV_HEADS, POOL_GROUP_DIM, POOL_OUT_GROUP), lambda i: (0, 0, 0)),
            pl.BlockSpec((1, D_MODEL), lambda i: (0, 0)),
        ],
        out_specs=[
            pl.BlockSpec((rows, D_MODEL), row_blk),
            pl.BlockSpec((SAMPLE_DB, WINDOW, KV_DIM), db_blk),
            pl.BlockSpec((SAMPLE_DB, WINDOW, KV_DIM), db_blk),
            pl.BlockSpec((SAMPLE_DB, POOL_HIST, POOL_DIM), db_blk),
        ],
        out_shape=[
            jax.ShapeDtypeStruct((m, D_MODEL), F32),
            jax.ShapeDtypeStruct((db, WINDOW, KV_DIM), F32),
            jax.ShapeDtypeStruct((db, WINDOW, KV_DIM), F32),
            jax.ShapeDtypeStruct((db, POOL_HIST, POOL_DIM), F32),
        ],
        scratch_shapes=[
            pltpu.VMEM((SAMPLE_DB, 24, POOL_DIM), F32),
            pltpu.VMEM((rows, POOL_DIM), F32),
            pltpu.VMEM((rows, D_MODEL), F32),
        ],
        compiler_params=pltpu.CompilerParams(
            dimension_semantics=("arbitrary",), vmem_limit_bytes=VMEM_LIMIT),
        name="mixer_sample_split" if q.dtype == F32 else "mixer_sample",
    )(sinks, q, kn, vn, u, ga, gp, ck, cv, pool, wpool, pscale)


PROJ_TN = 512


def _proj_residual_kernel(a_ref, x_ref, w_ref, o_ref):
    a = a_ref[...]
    a_hi, a_lo = _split(a) if w_ref.dtype == F32 else (a.astype(BF16), None)
    o_ref[...] = x_ref[...] + _mm(a_hi, a_lo, w_ref[...])


def _proj_residual(a, x, w):
    m = x.shape[0]
    return pl.pallas_call(
        _proj_residual_kernel,
        grid=(D_MODEL // PROJ_TN,),
        in_specs=[
            pl.BlockSpec((m, D_MODEL), lambda j: (0, 0)),
            pl.BlockSpec((m, PROJ_TN), lambda j: (0, j)),
            pl.BlockSpec((D_MODEL, PROJ_TN), lambda j: (0, j)),
        ],
        out_specs=pl.BlockSpec((m, PROJ_TN), lambda j: (0, j)),
        out_shape=jax.ShapeDtypeStruct((m, D_MODEL), F32),
        compiler_params=pltpu.CompilerParams(
            dimension_semantics=("parallel",), vmem_limit_bytes=VMEM_LIMIT),
        name="proj_residual_split" if w.dtype == F32 else "proj_residual",
    )(a, x, w)


FFN_TF = 512


def _ffn_kernel(x_ref, g_ref, wr_ref, br_ref, wg_ref, wu_ref, wd_ref, o_ref, xhi_sc, xlo_sc, comb_sc, *, moe):
    e = pl.program_id(1)
    f = pl.program_id(2)
    precise = wg_ref.dtype == F32

    @pl.when((e == 0) & (f == 0))
    def _():
        x = x_ref[...]
        xn = _rms_norm(x, g_ref[...])
        o_ref[...] = x
        if precise or moe:
            xn_hi, xn_lo = _split(xn)
            xhi_sc[...] = xn_hi
            if precise:
                xlo_sc[...] = xn_lo
        else:
            xhi_sc[...] = xn.astype(BF16)
        if moe:
            logits = _mm(xn_hi, xn_lo, wr_ref[...]) + br_ref[...]
            lane = lax.broadcasted_iota(jnp.int32, logits.shape, 1)
            v1 = jnp.max(logits, axis=-1, keepdims=True)
            i1 = jnp.min(jnp.where(logits == v1, lane, LANES), axis=-1, keepdims=True)
            rest = jnp.where(lane == i1, NEG, logits)
            v2 = jnp.max(rest, axis=-1, keepdims=True)
            i2 = jnp.min(jnp.where(rest == v2, lane, LANES), axis=-1, keepdims=True)
            e2 = jnp.exp(v2 - v1)
            g1 = 1.0 / (1.0 + e2)
            g2 = e2 / (1.0 + e2)
            comb_sc[...] = jnp.where(lane == i1, g1, 0.0) + jnp.where(lane == i2, g2, 0.0)

    xn_hi = xhi_sc[...]
    xn_lo = xlo_sc[...] if precise else None
    gate = _mm(xn_hi, xn_lo, wg_ref[0])
    up = _mm(xn_hi, xn_lo, wu_ref[0])
    h = gate * _sigmoid(gate) * up
    if moe:
        lane = lax.broadcasted_iota(jnp.int32, comb_sc.shape, 1)
        h = h * jnp.sum(jnp.where(lane == e, comb_sc[...], 0.0), axis=-1, keepdims=True)
    h_hi, h_lo = _split(h) if precise else (h.astype(BF16), None)
    o_ref[...] += _mm(h_hi, h_lo, wd_ref[0])


def _ffn(x, g, wr, br, wg, wu, wd, *, moe, tm):
    m = x.shape[0]
    n_e, _, d_f = wg.shape
    n_f = d_f // FFN_TF
    lo_rows = tm if wg.dtype == F32 else 16
    return pl.pallas_call(
        functools.partial(_ffn_kernel, moe=moe),
        grid=(m // tm, n_e, n_f),
        in_specs=[
            pl.BlockSpec((tm, D_MODEL), lambda i, e, f: (i, 0)),
            pl.BlockSpec((1, D_MODEL), lambda i, e, f: (0, 0)),
            pl.BlockSpec((D_MODEL, LANES), lambda i, e, f: (0, 0)),
            pl.BlockSpec((1, LANES), lambda i, e, f: (0, 0)),
            pl.BlockSpec((1, D_MODEL, FFN_TF), lambda i, e, f: (e, 0, f)),
            pl.BlockSpec((1, D_MODEL, FFN_TF), lambda i, e, f: (e, 0, f)),
            pl.BlockSpec((1, FFN_TF, D_MODEL), lambda i, e, f: (e, f, 0)),
        ],
        out_specs=pl.BlockSpec((tm, D_MODEL), lambda i, e, f: (i, 0)),
        out_shape=jax.ShapeDtypeStruct((m, D_MODEL), F32),
        scratch_shapes=[pltpu.VMEM((tm, D_MODEL), BF16), pltpu.VMEM((lo_rows, D_MODEL), BF16),
                        pltpu.VMEM((tm, LANES), F32)],
        compiler_params=pltpu.CompilerParams(
            dimension_semantics=("parallel", "arbitrary", "arbitrary"), vmem_limit_bytes=VMEM_LIMIT),
        name=("ffn_moe" if moe else "ffn_dense") + ("_split" if wg.dtype == F32 else ""),
    )(x, g, wr, br, wg, wu, wd)


def _final_norm_kernel(x_ref, g_ref, o_ref):
    o_ref[...] = _rms_norm(x_ref[...], g_ref[...])


def _final_norm(x, g, *, tm):
    m = x.shape[0]
    return pl.pallas_call(
        _final_norm_kernel,
        grid=(m // tm,),
        in_specs=[pl.BlockSpec((tm, D_MODEL), lambda i: (i, 0)), pl.BlockSpec((1, D_MODEL), lambda i: (0, 0))],
        out_specs=pl.BlockSpec((tm, D_MODEL), lambda i: (i, 0)),
        out_shape=jax.ShapeDtypeStruct((m, D_MODEL), F32),
        compiler_params=pltpu.CompilerParams(dimension_semantics=("parallel",), vmem_limit_bytes=VMEM_LIMIT),
        name="final_norm",
    )(x, g)


def _rope_tables(pos):
    half = HEAD_DIM // 2
    inv = ROPE_THETA ** (-2.0 * jnp.arange(half, dtype=F32) / HEAD_DIM)
    ang = pos.astype(F32)[:, None] * inv[None, :]
    cos, sin = jnp.cos(ang), jnp.sin(ang)
    reps = LANES // HEAD_DIM
    return (jnp.tile(jnp.concatenate([cos, cos], axis=-1), (1, reps)),
            jnp.tile(jnp.concatenate([-sin, sin], axis=-1), (1, reps)))


def _row_tile(m, cap):
    t = cap
    while m % t:
        t //= 2
    return t


def _put_tail(main, tail, batch, seq):
    c = main.shape[1]
    upd = tail.reshape(batch, BLOCK, c).astype(main.dtype)
    return lax.dynamic_update_slice(main.reshape(batch, seq, c), upd, (0, seq - BLOCK, 0)).reshape(batch * seq, c)


def kernel(x_prompt, x_sample, cache_k, cache_v, state_pool, norm_mix, norm_ffn, norm_final, w_in, attn_sinks,
           w_pool_map, pool_scale, w_out, w_ffn_gate, w_ffn_up, w_ffn_down, w_router, b_router, w_exp_gate,
           w_exp_up, w_exp_down):
    batch, seq, _ = x_prompt.shape
    dec_batch, dec_seq, _ = x_sample.shape
    depth = w_in.shape[0]
    assert seq % BLOCK == 0 and dec_batch % SAMPLE_DB == 0 and dec_seq <= POOL_HIST

    w_in_b = w_in.astype(BF16)
    w_pool_b = w_pool_map.astype(BF16)
    w_out_b = w_out.astype(BF16)
    w_fg_b, w_fu_b, w_fd_b = (w.astype(BF16)[:, None] for w in (w_ffn_gate, w_ffn_up, w_ffn_down))
    w_eg_b, w_eu_b, w_ed_b = (w.astype(BF16) for w in (w_exp_gate, w_exp_up, w_exp_down))
    w_r = jnp.pad(w_router.astype(F32), ((0, 0), (0, 0), (0, LANES - N_EXPERTS)))
    b_r = jnp.pad(b_router.astype(F32), ((0, 0), (0, LANES - N_EXPERTS)), constant_values=NEG)[:, None]
    zero_wr = jnp.zeros((D_MODEL, LANES), F32)
    zero_br = jnp.zeros((1, LANES), F32)

    pos_p = jnp.arange(seq, dtype=jnp.int32)
    pos_s = jnp.tile(PAST_LEN + jnp.arange(dec_seq, dtype=jnp.int32), dec_batch)
    cos_p, sin_p = _rope_tables(pos_p)
    cos_s, sin_s = _rope_tables(pos_s)
    cos_x, sin_x = _rope_tables(jnp.concatenate([jnp.tile(pos_p[seq - BLOCK:], batch), pos_s]))

    m_p, m_s, m_t = batch * seq, dec_batch * dec_seq, batch * BLOCK
    tm_in_p = _row_tile(seq, 512)
    tm_ffn_p = _row_tile(m_p, 512)

    xp = x_prompt.reshape(m_p, D_MODEL)
    xs = x_sample.reshape(m_s, D_MODEL)
    ck = cache_k.reshape(depth, dec_batch, WINDOW, KV_DIM)
    cv = cache_v.reshape(depth, dec_batch, WINDOW, KV_DIM)

    n_split = min(2, depth)
    xx = jnp.concatenate([x_prompt[:, seq - BLOCK:].reshape(m_t, D_MODEL), xs], axis=0)

    kp_l, vp_l, pp_l, ks_l, vs_l, ps_l = [], [], [], [], [], []
    for l in range(depth):
        g_mix = norm_mix[l][None]
        g_ffn = norm_ffn[l][None]
        pscale = pool_scale[l][None]
        sinks = attn_sinks[l]
        split = l < n_split

        q, k, v, u, ga, gp = _inproj(xp, g_mix, w_in_b[l], cos_p, sin_p, tm=tm_in_p)
        if split:
            qx, kx, vx, ux, gax, gpx = _inproj(xx, g_mix, w_in[l], cos_x, sin_x, tm=m_t + m_s)
            k, v, u = (_put_tail(a, b[:m_t], batch, seq) for a, b in ((k, kx), (v, vx), (u, ux)))
        xp = _mixer_prompt(q, k, v, u, ga, gp, xp, sinks, w_pool_b[l], pscale, w_out_b[l], batch=batch, seq=seq)
        if split:
            mix_t = _mixer_prompt_tail(qx[:m_t], k, v, u, gax[:m_t], gpx[:m_t], sinks, w_pool_map[l], pscale,
                                       batch=batch, seq=seq)
            mix_s, nk, nv, npool = _mixer_sample(qx[m_t:], kx[m_t:], vx[m_t:], ux[m_t:], gax[m_t:], gpx[m_t:],
                                                 ck[l], cv[l], state_pool[l], sinks, w_pool_map[l], pscale,
                                                 dec_seq=dec_seq)
            xx = _proj_residual(jnp.concatenate([mix_t, mix_s], axis=0), xx, w_out[l])
            xp = _put_tail(xp, xx[:m_t], batch, seq)
        else:
            qs, ks, vs, us, gas, gps = _inproj(xs, g_mix, w_in_b[l], cos_s, sin_s, tm=m_s)
            mix_s, nk, nv, npool = _mixer_sample(qs, ks, vs, us, gas, gps, ck[l], cv[l], state_pool[l], sinks,
                                                 w_pool_b[l], pscale, dec_seq=dec_seq)
            xs = _proj_residual(mix_s, xs, w_out_b[l])
        kp_l.append(k.reshape(batch, seq, N_KV_HEADS, HEAD_DIM)[:, seq - WINDOW:])
        vp_l.append(v.reshape(batch, seq, N_KV_HEADS, HEAD_DIM)[:, seq - WINDOW:])
        pp_l.append(u.reshape(batch, seq, POOL_DIM)[:, seq - POOL_HIST:])
        ks_l.append(nk.reshape(dec_batch, WINDOW, N_KV_HEADS, HEAD_DIM))
        vs_l.append(nv.reshape(dec_batch, WINDOW, N_KV_HEADS, HEAD_DIM))
        ps_l.append(npool)

        i = l // 2
        if l % 2 == 0:
            xp = _ffn(xp, g_ffn, zero_wr, zero_br, w_fg_b[i], w_fu_b[i], w_fd_b[i], moe=False, tm=tm_ffn_p)
            if split and l + 1 < n_split:
                xx = _ffn(xx, g_ffn, zero_wr, zero_br, w_ffn_gate[i][None], w_ffn_up[i][None],
                          w_ffn_down[i][None], moe=False, tm=m_t + m_s)
                xp = _put_tail(xp, xx[:m_t], batch, seq)
            else:
                if split:
                    xs = xx[m_t:]
                xs = _ffn(xs, g_ffn, zero_wr, zero_br, w_fg_b[i], w_fu_b[i], w_fd_b[i], moe=False, tm=m_s)
        else:
            if split:
                xs = xx[m_t:]
            xp = _ffn(xp, g_ffn, w_r[i], b_r[i], w_eg_b[i], w_eu_b[i], w_ed_b[i], moe=True, tm=tm_ffn_p)
            xs = _ffn(xs, g_ffn, w_r[i], b_r[i], w_eg_b[i], w_eu_b[i], w_ed_b[i], moe=True, tm=m_s)

    g_fin = norm_final[None]
    y_p = _final_norm(xp, g_fin, tm=tm_ffn_p).reshape(batch, seq, D_MODEL)
    y_s = _final_norm(xs, g_fin, tm=m_s).reshape(dec_batch, dec_seq, D_MODEL)
    return (y_p, y_s, jnp.stack(kp_l), jnp.stack(vp_l), jnp.stack(pp_l),
            jnp.stack(ks_l), jnp.stack(vs_l), jnp.stack(ps_l))
```

```python
import functools

import jax
import jax.numpy as jnp
from jax import lax
from jax.experimental import pallas as pl
from jax.experimental.pallas import tpu as pltpu

D_MODEL = 2048
HEAD_DIM = 64
N_HEADS = 32
N_KV_HEADS = 4
GROUP = N_HEADS // N_KV_HEADS
KV_DIM = N_KV_HEADS * HEAD_DIM
WINDOW = 128
BLOCK = 128
ROPE_THETA = 10000.0
PAST_LEN = 16384
POOL_WINDOWS = (2, 4, 8, 16)
POOL_GROUP_DIM = 256
POOL_DIM = 1024
POOL_OUT_GROUP = 512
POOL_HIST = 15
N_EXPERTS = 8
EPS = 1e-5
NEG = -1e30

BF16 = jnp.bfloat16
F32 = jnp.float32

LANES = 128
VMEM_LIMIT = 52 * 1024 * 1024

IN_TN = 512
IN_Q_TILES = D_MODEL // IN_TN
IN_KV_TILE = IN_Q_TILES
IN_U_START = IN_KV_TILE + 1
IN_U_TILES = POOL_DIM // IN_TN
IN_GA_START = IN_U_START + IN_U_TILES
IN_G_TILES = D_MODEL // IN_TN
IN_GP_START = IN_GA_START + IN_G_TILES
IN_TILES = IN_GP_START + IN_G_TILES


def _sigmoid(x):
    return 1.0 / (1.0 + jnp.exp(-x))


def _split(a):
    hi = a.astype(BF16)
    lo = (a - hi.astype(F32)).astype(BF16)
    return hi, lo


def _dot(a, b):
    return jnp.dot(a, b, preferred_element_type=F32)


def _dot_nt(a, b):
    return lax.dot_general(a, b, (((1,), (1,)), ((), ())), preferred_element_type=F32)


def _mm(a_hi, a_lo, w):
    if w.dtype == BF16:
        return _dot(a_hi, w)
    w_hi, w_lo = _split(w)
    return _dot(a_hi, w_hi) + (_dot(a_lo, w_hi) + _dot(a_hi, w_lo))


def _qk(q, k_hi, k_lo):
    if q.dtype == BF16:
        return _dot_nt(q, k_hi)
    q_hi, q_lo = _split(q)
    return _dot_nt(q_hi, k_hi) + (_dot_nt(q_lo, k_hi) + _dot_nt(q_hi, k_lo))


def _pv(p, v_hi, v_lo):
    if v_lo is None:
        return _dot(p.astype(BF16), v_hi)
    p_hi, p_lo = _split(p)
    return _dot(p_hi, v_hi) + (_dot(p_lo, v_hi) + _dot(p_hi, v_lo))


def _kv_operands(t, precise):
    return _split(t) if precise else (t.astype(BF16), None)


def _cols(t, c):
    return None if t is None else t[:, c]


def _pool_map(pm, w_ref, gi):
    pm_hi, pm_lo = _split(pm) if w_ref.dtype == F32 else (pm.astype(BF16), None)
    return _mm(pm_hi, pm_lo, w_ref[gi])


def _rms_norm(x, g):
    ms = jnp.mean(x * x, axis=-1, keepdims=True)
    return x * lax.rsqrt(ms + EPS) * g


def _rope(t, cos, sin_signed):
    width = t.shape[1]
    lane = lax.broadcasted_iota(jnp.int32, t.shape, 1)
    first_half = (lane % HEAD_DIM) < (HEAD_DIM // 2)
    rot = jnp.where(first_half, pltpu.roll(t, width - HEAD_DIM // 2, 1), pltpu.roll(t, HEAD_DIM // 2, 1))
    reps = width // cos.shape[1]
    return t * jnp.tile(cos, (1, reps)) + rot * jnp.tile(sin_signed, (1, reps))


def _inproj_kernel(x_ref, g_ref, w_ref, cos_ref, sin_ref, q_ref, k_ref, v_ref, u_ref, ga_ref, gp_ref,
                   xhi_sc, xlo_sc):
    j = pl.program_id(1)
    precise = w_ref.dtype == F32

    @pl.when(j == 0)
    def _():
        xn = _rms_norm(x_ref[...], g_ref[...])
        if precise:
            xhi_sc[...], xlo_sc[...] = _split(xn)
        else:
            xhi_sc[...] = xn.astype(BF16)

    z = _mm(xhi_sc[...], xlo_sc[...] if precise else None, w_ref[...])

    @pl.when(j < IN_Q_TILES)
    def _():
        q_ref[...] = (_rope(z, cos_ref[...], sin_ref[...]) * (HEAD_DIM ** -0.5)).astype(q_ref.dtype)

    @pl.when(j == IN_KV_TILE)
    def _():
        k_ref[...] = _rope(z[:, :KV_DIM], cos_ref[...], sin_ref[...])
        v_ref[...] = z[:, KV_DIM:]

    @pl.when((j >= IN_U_START) & (j < IN_GA_START))
    def _():
        u_ref[...] = z

    @pl.when((j >= IN_GA_START) & (j < IN_GP_START))
    def _():
        ga_ref[...] = z

    @pl.when(j >= IN_GP_START)
    def _():
        gp_ref[...] = z


def _inproj(x, g, w, cos, sin, *, tm):
    m = x.shape[0]
    pos_blocks = cos.shape[0] // tm
    grid = (m // tm, IN_TILES)

    def clip(j, lo, n):
        return jnp.clip(j - lo, 0, n - 1)

    lo_rows = tm if w.dtype == F32 else 16
    return pl.pallas_call(
        _inproj_kernel,
        grid=grid,
        in_specs=[
            pl.BlockSpec((tm, D_MODEL), lambda i, j: (i, 0)),
            pl.BlockSpec((1, D_MODEL), lambda i, j: (0, 0)),
            pl.BlockSpec((D_MODEL, IN_TN), lambda i, j: (0, j)),
            pl.BlockSpec((tm, LANES), lambda i, j: (i % pos_blocks, 0)),
            pl.BlockSpec((tm, LANES), lambda i, j: (i % pos_blocks, 0)),
        ],
        out_specs=[
            pl.BlockSpec((tm, IN_TN), lambda i, j: (i, clip(j, 0, IN_Q_TILES))),
            pl.BlockSpec((tm, KV_DIM), lambda i, j: (i, 0)),
            pl.BlockSpec((tm, KV_DIM), lambda i, j: (i, 0)),
            pl.BlockSpec((tm, IN_TN), lambda i, j: (i, clip(j, IN_U_START, IN_U_TILES))),
            pl.BlockSpec((tm, IN_TN), lambda i, j: (i, clip(j, IN_GA_START, IN_G_TILES))),
            pl.BlockSpec((tm, IN_TN), lambda i, j: (i, clip(j, IN_GP_START, IN_G_TILES))),
        ],
        out_shape=[
            jax.ShapeDtypeStruct((m, D_MODEL), w.dtype),
            jax.ShapeDtypeStruct((m, KV_DIM), F32),
            jax.ShapeDtypeStruct((m, KV_DIM), F32),
            jax.ShapeDtypeStruct((m, POOL_DIM), F32),
            jax.ShapeDtypeStruct((m, D_MODEL), F32),
            jax.ShapeDtypeStruct((m, D_MODEL), F32),
        ],
        scratch_shapes=[pltpu.VMEM((tm, D_MODEL), BF16), pltpu.VMEM((lo_rows, D_MODEL), BF16)],
        compiler_params=pltpu.CompilerParams(
            dimension_semantics=("parallel", "arbitrary"), vmem_limit_bytes=VMEM_LIMIT),
        name="inproj_split" if w.dtype == F32 else "inproj",
    )(x, g, w, cos, sin)


def _mixer_prompt_kernel(sinks_ref, q_ref, kc_ref, kp_ref, vc_ref, vp_ref, uc_ref, up_ref, ga_ref, gp_ref,
                         wpool_ref, pscale_ref, *rest, first_block):
    precise = q_ref.dtype == F32
    if precise:
        mix_ref, ext_sc = rest
    else:
        x_ref, wout_ref, o_ref, ext_sc, mix_ref = rest
    n = first_block + pl.program_id(1)
    not_first = n > 0

    k_hi, k_lo = _kv_operands(jnp.concatenate([kp_ref[...], kc_ref[...]], axis=0), precise)
    v_hi, v_lo = _kv_operands(jnp.concatenate([vp_ref[...], vc_ref[...]], axis=0), precise)
    qi = lax.broadcasted_iota(jnp.int32, (BLOCK, 2 * BLOCK), 0)
    kj = lax.broadcasted_iota(jnp.int32, (BLOCK, 2 * BLOCK), 1)
    dist = qi + BLOCK - kj
    mask = (dist >= 0) & (dist <= WINDOW) & ((kj >= BLOCK) | not_first)

    ext_sc[0:16, :] = jnp.where(not_first, up_ref[BLOCK - 16:BLOCK, :], 0.0)
    ext_sc[16:16 + BLOCK, :] = uc_ref[...]
    pos = n * BLOCK + lax.broadcasted_iota(jnp.int32, (BLOCK, 1), 0)

    for kh in range(N_KV_HEADS):
        w = POOL_WINDOWS[kh]
        pcs = slice(kh * POOL_GROUP_DIM, (kh + 1) * POOL_GROUP_DIM)
        cur = ext_sc[16:16 + BLOCK, pcs]
        acc = cur
        for d in range(1, w):
            acc = acc + ext_sc[16 - d:16 - d + BLOCK, pcs]
        cnt = jnp.minimum(pos + 1, w).astype(F32)
        pm = acc / cnt - cur
        ocs = slice(kh * POOL_OUT_GROUP, (kh + 1) * POOL_OUT_GROUP)
        pg = _pool_map(pm, wpool_ref, kh) * pscale_ref[:, ocs]

        hs = slice(kh * HEAD_DIM, (kh + 1) * HEAD_DIM)
        parts = []
        for g in range(GROUP):
            h = kh * GROUP + g
            s = _qk(q_ref[:, h * HEAD_DIM:(h + 1) * HEAD_DIM], k_hi[:, hs], _cols(k_lo, hs))
            s = jnp.where(mask, s, NEG)
            sink = sinks_ref[h]
            m = jnp.maximum(jnp.max(s, axis=-1, keepdims=True), sink)
            p = jnp.exp(s - m)
            denom = jnp.sum(p, axis=-1, keepdims=True) + jnp.exp(sink - m)
            parts.append(_pv(p, v_hi[:, hs], _cols(v_lo, hs)) / denom)
        a = jnp.concatenate(parts, axis=1)
        mix = _sigmoid(ga_ref[:, ocs]) * a + _sigmoid(gp_ref[:, ocs]) * pg
        mix_ref[:, ocs] = mix.astype(mix_ref.dtype)

    if not precise:
        o_ref[...] = x_ref[...] + _dot(mix_ref[...], wout_ref[...])


def _mixer_prompt(q, k, v, u, ga, gp, x, sinks, wpool, pscale, wout, *, batch, seq):
    nb = seq // BLOCK
    m = batch * seq

    def cur(b, n):
        return (b * nb + n, 0)

    def prev(b, n):
        return (jnp.maximum(b * nb + n - 1, 0), 0)

    return pl.pallas_call(
        functools.partial(_mixer_prompt_kernel, first_block=0),
        grid=(batch, nb),
        in_specs=[
            pl.BlockSpec(memory_space=pltpu.SMEM),
            pl.BlockSpec((BLOCK, D_MODEL), cur),
            pl.BlockSpec((BLOCK, KV_DIM), cur),
            pl.BlockSpec((BLOCK, KV_DIM), prev),
            pl.BlockSpec((BLOCK, KV_DIM), cur),
            pl.BlockSpec((BLOCK, KV_DIM), prev),
            pl.BlockSpec((BLOCK, POOL_DIM), cur),
            pl.BlockSpec((BLOCK, POOL_DIM), prev),
            pl.BlockSpec((BLOCK, D_MODEL), cur),
            pl.BlockSpec((BLOCK, D_MODEL), cur),
            pl.BlockSpec((N_KV_HEADS, POOL_GROUP_DIM, POOL_OUT_GROUP), lambda b, n: (0, 0, 0)),
            pl.BlockSpec((1, D_MODEL), lambda b, n: (0, 0)),
            pl.BlockSpec((BLOCK, D_MODEL), cur),
            pl.BlockSpec((D_MODEL, D_MODEL), lambda b, n: (0, 0)),
        ],
        out_specs=pl.BlockSpec((BLOCK, D_MODEL), cur),
        out_shape=jax.ShapeDtypeStruct((m, D_MODEL), F32),
        scratch_shapes=[pltpu.VMEM((16 + BLOCK, POOL_DIM), F32), pltpu.VMEM((BLOCK, D_MODEL), BF16)],
        compiler_params=pltpu.CompilerParams(
            dimension_semantics=("parallel", "arbitrary"), vmem_limit_bytes=VMEM_LIMIT),
        name="mixer_prompt",
    )(sinks, q, k, k, v, v, u, u, ga, gp, wpool, pscale, x, wout)


def _mixer_prompt_tail(q_t, k, v, u, ga_t, gp_t, sinks, wpool, pscale, *, batch, seq):
    nb = seq // BLOCK

    def tail(b, n):
        return (b, 0)

    def cur(b, n):
        return (b * nb + nb - 1, 0)

    def prev(b, n):
        return (jnp.maximum(b * nb + nb - 2, 0), 0)

    return pl.pallas_call(
        functools.partial(_mixer_prompt_kernel, first_block=nb - 1),
        grid=(batch, 1),
        in_specs=[
            pl.BlockSpec(memory_space=pltpu.SMEM),
            pl.BlockSpec((BLOCK, D_MODEL), tail),
            pl.BlockSpec((BLOCK, KV_DIM), cur),
            pl.BlockSpec((BLOCK, KV_DIM), prev),
            pl.BlockSpec((BLOCK, KV_DIM), cur),
            pl.BlockSpec((BLOCK, KV_DIM), prev),
            pl.BlockSpec((BLOCK, POOL_DIM), cur),
            pl.BlockSpec((BLOCK, POOL_DIM), prev),
            pl.BlockSpec((BLOCK, D_MODEL), tail),
            pl.BlockSpec((BLOCK, D_MODEL), tail),
            pl.BlockSpec((N_KV_HEADS, POOL_GROUP_DIM, POOL_OUT_GROUP), lambda b, n: (0, 0, 0)),
            pl.BlockSpec((1, D_MODEL), lambda b, n: (0, 0)),
        ],
        out_specs=pl.BlockSpec((BLOCK, D_MODEL), tail),
        out_shape=jax.ShapeDtypeStruct((batch * BLOCK, D_MODEL), F32),
        scratch_shapes=[pltpu.VMEM((16 + BLOCK, POOL_DIM), F32)],
        compiler_params=pltpu.CompilerParams(
            dimension_semantics=("parallel", "arbitrary"), vmem_limit_bytes=VMEM_LIMIT),
        name="mixer_prompt_tail",
    )(sinks, q_t, k, k, v, v, u, u, ga_t, gp_t, wpool, pscale)


SAMPLE_DB = 8


def _mixer_sample_kernel(sinks_ref, q_ref, kn_ref, vn_ref, u_ref, ga_ref, gp_ref, ck_ref, cv_ref, pool_ref,
                         wpool_ref, pscale_ref, mix_ref, nk_ref, nv_ref, npool_ref, ext_sc, pm_sc, a_sc,
                         *, dec_seq):
    t_new = dec_seq
    rows = SAMPLE_DB * t_new
    precise = q_ref.dtype == F32
    kc_hi, kc_lo = _kv_operands(ck_ref[...].reshape(SAMPLE_DB * WINDOW, KV_DIM), precise)
    vc_hi, vc_lo = _kv_operands(cv_ref[...].reshape(SAMPLE_DB * WINDOW, KV_DIM), precise)
    kn_hi, kn_lo = _kv_operands(kn_ref[...], precise)
    vn_hi, vn_lo = _kv_operands(vn_ref[...], precise)

    rho = lax.broadcasted_iota(jnp.int32, (GROUP * rows, 1), 0)
    q_db = (rho % rows) // t_new
    q_t = rho % t_new
    q_g = rho // rows
    colc = lax.broadcasted_iota(jnp.int32, (1, SAMPLE_DB * WINDOW), 1)
    mask_c = (q_db == colc // WINDOW) & ((colc % WINDOW) >= q_t)
    coln = lax.broadcasted_iota(jnp.int32, (1, rows), 1)
    mask_n = (q_db == coln // t_new) & ((coln % t_new) <= q_t)

    for d in range(SAMPLE_DB):
        rs = slice(d * t_new, (d + 1) * t_new)
        ext_sc[d, 0:POOL_HIST, :] = pool_ref[d]
        ext_sc[d, POOL_HIST:POOL_HIST + t_new, :] = u_ref[rs, :]
        npool_ref[d] = ext_sc[d, t_new:t_new + POOL_HIST, :]
        nk_ref[d, 0:WINDOW - t_new, :] = ck_ref[d, t_new:WINDOW, :]
        nk_ref[d, WINDOW - t_new:WINDOW, :] = kn_ref[rs, :]
        nv_ref[d, 0:WINDOW - t_new, :] = cv_ref[d, t_new:WINDOW, :]
        nv_ref[d, WINDOW - t_new:WINDOW, :] = vn_ref[rs, :]
        for gi, w in enumerate(POOL_WINDOWS):
            pcs = slice(gi * POOL_GROUP_DIM, (gi + 1) * POOL_GROUP_DIM)
            cur = ext_sc[d, POOL_HIST:POOL_HIST + t_new, pcs]
            acc = cur
            for j in range(1, w):
                acc = acc + ext_sc[d, POOL_HIST - j:POOL_HIST - j + t_new, pcs]
            pm_sc[rs, pcs] = acc / float(w) - cur

    for kh in range(N_KV_HEADS):
        hs = slice(kh * HEAD_DIM, (kh + 1) * HEAD_DIM)
        qs = jnp.concatenate(
            [q_ref[:, (kh * GROUP + g) * HEAD_DIM:(kh * GROUP + g + 1) * HEAD_DIM] for g in range(GROUP)], axis=0)
        sc = jnp.where(mask_c, _qk(qs, kc_hi[:, hs], _cols(kc_lo, hs)), NEG)
        sn = jnp.where(mask_n, _qk(qs, kn_hi[:, hs], _cols(kn_lo, hs)), NEG)
        sink = jnp.zeros((GROUP * rows, 1), F32)
        for g in range(GROUP):
            sink = jnp.where(q_g == g, sinks_ref[kh * GROUP + g], sink)
        m = jnp.maximum(jnp.maximum(jnp.max(sc, axis=-1, keepdims=True), jnp.max(sn, axis=-1, keepdims=True)), sink)
        pc = jnp.exp(sc - m)
        pn = jnp.exp(sn - m)
        denom = jnp.sum(pc, axis=-1, keepdims=True) + jnp.sum(pn, axis=-1, keepdims=True) + jnp.exp(sink - m)
        o = (_pv(pc, vc_hi[:, hs], _cols(vc_lo, hs)) + _pv(pn, vn_hi[:, hs], _cols(vn_lo, hs))) / denom
        for g in range(GROUP):
            h = kh * GROUP + g
            a_sc[:, h * HEAD_DIM:(h + 1) * HEAD_DIM] = o[g * rows:(g + 1) * rows, :]

    for gi in range(N_KV_HEADS):
        pcs = slice(gi * POOL_GROUP_DIM, (gi + 1) * POOL_GROUP_DIM)
        ocs = slice(gi * POOL_OUT_GROUP, (gi + 1) * POOL_OUT_GROUP)
        pg = _pool_map(pm_sc[:, pcs], wpool_ref, gi) * pscale_ref[:, ocs]
        mix_ref[:, ocs] = _sigmoid(ga_ref[:, ocs]) * a_sc[:, ocs] + _sigmoid(gp_ref[:, ocs]) * pg


def _mixer_sample(q, kn, vn, u, ga, gp, ck, cv, pool, sinks, wpool, pscale, *, dec_seq):
    db = ck.shape[0]
    rows = SAMPLE_DB * dec_seq
    m = db * dec_seq
    row_blk = lambda i: (i, 0)
    db_blk = lambda i: (i, 0, 0)
    return pl.pallas_call(
        functools.partial(_mixer_sample_kernel, dec_seq=dec_seq),
        grid=(db // SAMPLE_DB,),
        in_specs=[
            pl.BlockSpec(memory_space=pltpu.SMEM),
            pl.BlockSpec((rows, D_MODEL), row_blk),
            pl.BlockSpec((rows, KV_DIM), row_blk),
            pl.BlockSpec((rows, KV_DIM), row_blk),
            pl.BlockSpec((rows, POOL_DIM), row_blk),
            pl.BlockSpec((rows, D_MODEL), row_blk),
            pl.BlockSpec((rows, D_MODEL), row_blk),
            pl.BlockSpec((SAMPLE_DB, WINDOW, KV_DIM), db_blk),
            pl.BlockSpec((SAMPLE_DB, WINDOW, KV_DIM), db_blk),
            pl.BlockSpec((SAMPLE_DB, POOL_HIST, POOL_DIM), db_blk),
            pl.BlockSpec((N_KV_HEADS, POOL_GROUP_DIM, POOL_OUT_GROUP), lambda i: (0, 0, 0)),
            pl.BlockSpec((1, D_MODEL), lambda i: (0, 0)),
        ],
        out_specs=[
            pl.BlockSpec((rows, D_MODEL), row_blk),
            pl.BlockSpec((SAMPLE_DB, WINDOW, KV_DIM), db_blk),
            pl.BlockSpec((SAMPLE_DB, WINDOW, KV_DIM), db_blk),
            pl.BlockSpec((SAMPLE_DB, POOL_HIST, POOL_DIM), db_blk),
        ],
        out_shape=[
            jax.ShapeDtypeStruct((m, D_MODEL), F32),
            jax.ShapeDtypeStruct((db, WINDOW, KV_DIM), F32),
            jax.ShapeDtypeStruct((db, WINDOW, KV_DIM), F32),
            jax.ShapeDtypeStruct((db, POOL_HIST, POOL_DIM), F32),
        ],
        scratch_shapes=[
            pltpu.VMEM((SAMPLE_DB, 24, POOL_DIM), F32),
            pltpu.VMEM((rows, POOL_DIM), F32),
            pltpu.VMEM((rows, D_MODEL), F32),
        ],
        compiler_params=pltpu.CompilerParams(
            dimension_semantics=("arbitrary",), vmem_limit_bytes=VMEM_LIMIT),
        name="mixer_sample_split" if q.dtype == F32 else "mixer_sample",
    )(sinks, q, kn, vn, u, ga, gp, ck, cv, pool, wpool, pscale)


PROJ_TN = 512


def _proj_residual_kernel(a_ref, x_ref, w_ref, o_ref):
    a = a_ref[...]
    a_hi, a_lo = _split(a) if w_ref.dtype == F32 else (a.astype(BF16), None)
    o_ref[...] = x_ref[...] + _mm(a_hi, a_lo, w_ref[...])


def _proj_residual(a, x, w):
    m = x.shape[0]
    return pl.pallas_call(
        _proj_residual_kernel,
        grid=(D_MODEL // PROJ_TN,),
        in_specs=[
            pl.BlockSpec((m, D_MODEL), lambda j: (0, 0)),
            pl.BlockSpec((m, PROJ_TN), lambda j: (0, j)),
            pl.BlockSpec((D_MODEL, PROJ_TN), lambda j: (0, j)),
        ],
        out_specs=pl.BlockSpec((m, PROJ_TN), lambda j: (0, j)),
        out_shape=jax.ShapeDtypeStruct((m, D_MODEL), F32),
        compiler_params=pltpu.CompilerParams(
            dimension_semantics=("parallel",), vmem_limit_bytes=VMEM_LIMIT),
        name="proj_residual_split" if w.dtype == F32 else "proj_residual",
    )(a, x, w)


FFN_TF = 512


def _ffn_kernel(x_ref, g_ref, wr_ref, br_ref, wg_ref, wu_ref, wd_ref, o_ref, xhi_sc, xlo_sc, comb_sc, *, moe):
    e = pl.program_id(1)
    f = pl.program_id(2)
    precise = wg_ref.dtype == F32

    @pl.when((e == 0) & (f == 0))
    def _():
        x = x_ref[...]
        xn = _rms_norm(x, g_ref[...])
        o_ref[...] = x
        if precise or moe:
            xn_hi, xn_lo = _split(xn)
            xhi_sc[...] = xn_hi
            if precise:
                xlo_sc[...] = xn_lo
        else:
            xhi_sc[...] = xn.astype(BF16)
        if moe:
            logits = _mm(xn_hi, xn_lo, wr_ref[...]) + br_ref[...]
            lane = lax.broadcasted_iota(jnp.int32, logits.shape, 1)
            v1 = jnp.max(logits, axis=-1, keepdims=True)
            i1 = jnp.min(jnp.where(logits == v1, lane, LANES), axis=-1, keepdims=True)
            rest = jnp.where(lane == i1, NEG, logits)
            v2 = jnp.max(rest, axis=-1, keepdims=True)
            i2 = jnp.min(jnp.where(rest == v2, lane, LANES), axis=-1, keepdims=True)
            e2 = jnp.exp(v2 - v1)
            g1 = 1.0 / (1.0 + e2)
            g2 = e2 / (1.0 + e2)
            comb_sc[...] = jnp.where(lane == i1, g1, 0.0) + jnp.where(lane == i2, g2, 0.0)

    xn_hi = xhi_sc[...]
    xn_lo = xlo_sc[...] if precise else None
    gate = _mm(xn_hi, xn_lo, wg_ref[0])
    up = _mm(xn_hi, xn_lo, wu_ref[0])
    h = gate * _sigmoid(gate) * up
    if moe:
        lane = lax.broadcasted_iota(jnp.int32, comb_sc.shape, 1)
        h = h * jnp.sum(jnp.where(lane == e, comb_sc[...], 0.0), axis=-1, keepdims=True)
    h_hi, h_lo = _split(h) if precise else (h.astype(BF16), None)
    o_ref[...] += _mm(h_hi, h_lo, wd_ref[0])


def _ffn(x, g, wr, br, wg, wu, wd, *, moe, tm):
    m = x.shape[0]
    n_e, _, d_f = wg.shape
    n_f = d_f // FFN_TF
    lo_rows = tm if wg.dtype == F32 else 16
    return pl.pallas_call(
        functools.partial(_ffn_kernel, moe=moe),
        grid=(m // tm, n_e, n_f),
        in_specs=[
            pl.BlockSpec((tm, D_MODEL), lambda i, e, f: (i, 0)),
            pl.BlockSpec((1, D_MODEL), lambda i, e, f: (0, 0)),
            pl.BlockSpec((D_MODEL, LANES), lambda i, e, f: (0, 0)),
            pl.BlockSpec((1, LANES), lambda i, e, f: (0, 0)),
            pl.BlockSpec((1, D_MODEL, FFN_TF), lambda i, e, f: (e, 0, f)),
            pl.BlockSpec((1, D_MODEL, FFN_TF), lambda i, e, f: (e, 0, f)),
            pl.BlockSpec((1, FFN_TF, D_MODEL), lambda i, e, f: (e, f, 0)),
        ],
        out_specs=pl.BlockSpec((tm, D_MODEL), lambda i, e, f: (i, 0)),
        out_shape=jax.ShapeDtypeStruct((m, D_MODEL), F32),
        scratch_shapes=[pltpu.VMEM((tm, D_MODEL), BF16), pltpu.VMEM((lo_rows, D_MODEL), BF16),
                        pltpu.VMEM((tm, LANES), F32)],
        compiler_params=pltpu.CompilerParams(
            dimension_semantics=("parallel", "arbitrary", "arbitrary"), vmem_limit_bytes=VMEM_LIMIT),
        name=("ffn_moe" if moe else "ffn_dense") + ("_split" if wg.dtype == F32 else ""),
    )(x, g, wr, br, wg, wu, wd)


MOE_TM = 512
MOE_ROUTE_TM = 512
MOE_MOVE_TM = 256


def _route_kernel(x_ref, g_ref, wr_ref, br_ref, o_ref):
    xn_hi, xn_lo = _split(_rms_norm(x_ref[...], g_ref[...]))
    logits = _mm(xn_hi, xn_lo, wr_ref[...]) + br_ref[...]
    lane = lax.broadcasted_iota(jnp.int32, logits.shape, 1)
    v1 = jnp.max(logits, axis=-1, keepdims=True)
    i1 = jnp.min(jnp.where(logits == v1, lane, LANES), axis=-1, keepdims=True)
    rest = jnp.where(lane == i1, NEG, logits)
    v2 = jnp.max(rest, axis=-1, keepdims=True)
    i2 = jnp.min(jnp.where(rest == v2, lane, LANES), axis=-1, keepdims=True)
    e2 = jnp.exp(v2 - v1)
    g1 = 1.0 / (1.0 + e2)
    g2 = e2 / (1.0 + e2)
    o_ref[...] = jnp.where(lane == 0, i1.astype(F32),
                           jnp.where(lane == 1, i2.astype(F32),
                                     jnp.where(lane == 2, g1, jnp.where(lane == 3, g2, 0.0))))


def _route(x, g, wr, br):
    m = x.shape[0]
    tm = _row_tile(m, MOE_ROUTE_TM)
    return pl.pallas_call(
        _route_kernel,
        grid=(m // tm,),
        in_specs=[
            pl.BlockSpec((tm, D_MODEL), lambda i: (i, 0)),
            pl.BlockSpec((1, D_MODEL), lambda i: (0, 0)),
            pl.BlockSpec((D_MODEL, LANES), lambda i: (0, 0)),
            pl.BlockSpec((1, LANES), lambda i: (0, 0)),
        ],
        out_specs=pl.BlockSpec((tm, LANES), lambda i: (i, 0)),
        out_shape=jax.ShapeDtypeStruct((m, LANES), F32),
        compiler_params=pltpu.CompilerParams(dimension_semantics=("parallel",), vmem_limit_bytes=VMEM_LIMIT),
        name="moe_route",
    )(x, g, wr, br)


def _moe_plan(route, n_tiles):
    experts = jnp.arange(N_EXPERTS, dtype=jnp.int32)
    e_pair = route[:, :2].astype(jnp.int32).reshape(-1)
    onehot = (e_pair[:, None] == experts[None, :]).astype(jnp.int32)
    csum = jnp.cumsum(onehot, axis=0)
    rank = jnp.sum(csum * onehot, axis=1) - 1
    tiles_e = (csum[-1] + MOE_TM - 1) // MOE_TM
    tend = jnp.cumsum(tiles_e)
    tstart = tend - tiles_e
    pos = (jnp.sum(onehot * tstart[None, :], axis=1) * MOE_TM + rank).astype(jnp.int32)
    n_used = tend[-1:].astype(jnp.int32)
    t_idx = jnp.arange(n_tiles, dtype=jnp.int32)
    tile_e = jnp.sum((t_idx[:, None] >= tend[None, :]).astype(jnp.int32), axis=1)
    last_e = jnp.max(jnp.where(tiles_e > 0, experts, 0))
    tile_e = jnp.minimum(tile_e, last_e).astype(jnp.int32)
    return pos, tile_e, n_used


def _row_copy(src, dst, sem):
    return pltpu.make_async_copy(src, dst, sem)


def _push_kernel(pos_ref, x_ref, g_ref, zeros_ref, xs_ref, xn_sc, sem):
    del zeros_ref
    tm = x_ref.shape[0]
    base = pl.program_id(0) * tm
    xn_sc[...] = _rms_norm(x_ref[...], g_ref[...])

    def start(r, carry):
        for c in range(2):
            dst = pos_ref[2 * (base + r) + c]
            _row_copy(xn_sc.at[pl.ds(r, 1)], xs_ref.at[pl.ds(dst, 1)], sem).start()
        return carry

    lax.fori_loop(0, tm, start, 0, unroll=8)

    def wait(r, carry):
        for c in range(2):
            _row_copy(xn_sc.at[pl.ds(r, 1)], xs_ref.at[pl.ds(0, 1)], sem).wait()
        return carry

    lax.fori_loop(0, tm, wait, 0, unroll=8)


def _push(x, g, pos, n_rows):
    m = x.shape[0]
    tm = _row_tile(m, MOE_MOVE_TM)
    zeros = jnp.zeros((n_rows, D_MODEL), F32)
    return pl.pallas_call(
        _push_kernel,
        grid_spec=pltpu.PrefetchScalarGridSpec(
            num_scalar_prefetch=1,
            grid=(m // tm,),
            in_specs=[
                pl.BlockSpec((tm, D_MODEL), lambda i, pos: (i, 0)),
                pl.BlockSpec((1, D_MODEL), lambda i, pos: (0, 0)),
                pl.BlockSpec(memory_space=pl.ANY),
            ],
            out_specs=pl.BlockSpec(memory_space=pl.ANY),
            scratch_shapes=[pltpu.VMEM((tm, D_MODEL), F32), pltpu.SemaphoreType.DMA],
        ),
        out_shape=jax.ShapeDtypeStruct((n_rows, D_MODEL), F32),
        input_output_aliases={3: 0},
        compiler_params=pltpu.CompilerParams(dimension_semantics=("arbitrary",), vmem_limit_bytes=VMEM_LIMIT),
        name="moe_push",
    )(pos, x, g, zeros)


def _experts_kernel(tile_e_ref, n_used_ref, x_ref, wg_ref, wu_ref, wd_ref, o_ref, xb_sc):
    t = pl.program_id(0)
    f = pl.program_id(1)
    used = t < n_used_ref[0]

    @pl.when(f == 0)
    def _():
        xb_sc[...] = x_ref[...].astype(BF16)
        o_ref[...] = jnp.zeros_like(o_ref)

    @pl.when(used)
    def _():
        xn = xb_sc[...]
        gate = _dot(xn, wg_ref[0])
        up = _dot(xn, wu_ref[0])
        h = gate * _sigmoid(gate) * up
        o_ref[...] += _dot(h.astype(BF16), wd_ref[0])


def _experts(xs, tile_e, n_used, wg, wu, wd):
    n_rows = xs.shape[0]
    n_f = wg.shape[2] // FFN_TF
    return pl.pallas_call(
        _experts_kernel,
        grid_spec=pltpu.PrefetchScalarGridSpec(
            num_scalar_prefetch=2,
            grid=(n_rows // MOE_TM, n_f),
            in_specs=[
                pl.BlockSpec((MOE_TM, D_MODEL), lambda t, f, te, nu: (t, 0)),
                pl.BlockSpec((1, D_MODEL, FFN_TF), lambda t, f, te, nu: (te[t], 0, f)),
                pl.BlockSpec((1, D_MODEL, FFN_TF), lambda t, f, te, nu: (te[t], 0, f)),
                pl.BlockSpec((1, FFN_TF, D_MODEL), lambda t, f, te, nu: (te[t], f, 0)),
            ],
            out_specs=pl.BlockSpec((MOE_TM, D_MODEL), lambda t, f, te, nu: (t, 0)),
            scratch_shapes=[pltpu.VMEM((MOE_TM, D_MODEL), BF16)],
        ),
        out_shape=jax.ShapeDtypeStruct((n_rows, D_MODEL), F32),
        compiler_params=pltpu.CompilerParams(
            dimension_semantics=("arbitrary", "arbitrary"), vmem_limit_bytes=VMEM_LIMIT),
        name="moe_experts",
    )(tile_e, n_used, xs, wg, wu, wd)


def _combine_kernel(pos_ref, x_ref, route_ref, ys_ref, o_ref, y_sc, sem):
    tm = x_ref.shape[0]
    base = pl.program_id(0) * tm

    def start(r, carry):
        for c in range(2):
            src = pos_ref[2 * (base + r) + c]
            _row_copy(ys_ref.at[pl.ds(src, 1)], y_sc.at[c, pl.ds(r, 1)], sem).start()
        return carry

    lax.fori_loop(0, tm, start, 0, unroll=8)

    def wait(r, carry):
        for c in range(2):
            _row_copy(ys_ref.at[pl.ds(0, 1)], y_sc.at[c, pl.ds(r, 1)], sem).wait()
        return carry

    lax.fori_loop(0, tm, wait, 0, unroll=8)
    o_ref[...] = x_ref[...] + route_ref[:, 2:3] * y_sc[0] + route_ref[:, 3:4] * y_sc[1]


def _combine(x, route, pos, ys):
    m = x.shape[0]
    tm = _row_tile(m, MOE_MOVE_TM)
    return pl.pallas_call(
        _combine_kernel,
        grid_spec=pltpu.PrefetchScalarGridSpec(
            num_scalar_prefetch=1,
            grid=(m // tm,),
            in_specs=[
                pl.BlockSpec((tm, D_MODEL), lambda i, pos: (i, 0)),
                pl.BlockSpec((tm, LANES), lambda i, pos: (i, 0)),
                pl.BlockSpec(memory_space=pl.ANY),
            ],
            out_specs=pl.BlockSpec((tm, D_MODEL), lambda i, pos: (i, 0)),
            scratch_shapes=[pltpu.VMEM((2, tm, D_MODEL), F32), pltpu.SemaphoreType.DMA],
        ),
        out_shape=jax.ShapeDtypeStruct((m, D_MODEL), F32),
        compiler_params=pltpu.CompilerParams(dimension_semantics=("arbitrary",), vmem_limit_bytes=VMEM_LIMIT),
        name="moe_combine",
    )(pos, x, route, ys)


def _moe_routed(x, g, wr, br, wg, wu, wd):
    m = x.shape[0]
    n_tiles = (2 * m) // MOE_TM + N_EXPERTS
    route = _route(x, g, wr, br)
    pos, tile_e, n_used = _moe_plan(route, n_tiles)
    xs = _push(x, g, pos, n_tiles * MOE_TM)
    ys = _experts(xs, tile_e, n_used, wg, wu, wd)
    return _combine(x, route, pos, ys)


def _final_norm_kernel(x_ref, g_ref, o_ref):
    o_ref[...] = _rms_norm(x_ref[...], g_ref[...])


def _final_norm(x, g, *, tm):
    m = x.shape[0]
    return pl.pallas_call(
        _final_norm_kernel,
        grid=(m // tm,),
        in_specs=[pl.BlockSpec((tm, D_MODEL), lambda i: (i, 0)), pl.BlockSpec((1, D_MODEL), lambda i: (0, 0))],
        out_specs=pl.BlockSpec((tm, D_MODEL), lambda i: (i, 0)),
        out_shape=jax.ShapeDtypeStruct((m, D_MODEL), F32),
        compiler_params=pltpu.CompilerParams(dimension_semantics=("parallel",), vmem_limit_bytes=VMEM_LIMIT),
        name="final_norm",
    )(x, g)


def _rope_tables(pos):
    half = HEAD_DIM // 2
    inv = ROPE_THETA ** (-2.0 * jnp.arange(half, dtype=F32) / HEAD_DIM)
    ang = pos.astype(F32)[:, None] * inv[None, :]
    cos, sin = jnp.cos(ang), jnp.sin(ang)
    reps = LANES // HEAD_DIM
    return (jnp.tile(jnp.concatenate([cos, cos], axis=-1), (1, reps)),
            jnp.tile(jnp.concatenate([-sin, sin], axis=-1), (1, reps)))


def _row_tile(m, cap):
    t = cap
    while m % t:
        t //= 2
    return t


def _put_tail(main, tail, batch, seq):
    c = main.shape[1]
    upd = tail.reshape(batch, BLOCK, c).astype(main.dtype)
    return lax.dynamic_update_slice(main.reshape(batch, seq, c), upd, (0, seq - BLOCK, 0)).reshape(batch * seq, c)


def kernel(x_prompt, x_sample, cache_k, cache_v, state_pool, norm_mix, norm_ffn, norm_final, w_in, attn_sinks,
           w_pool_map, pool_scale, w_out, w_ffn_gate, w_ffn_up, w_ffn_down, w_router, b_router, w_exp_gate,
           w_exp_up, w_exp_down):
    batch, seq, _ = x_prompt.shape
    dec_batch, dec_seq, _ = x_sample.shape
    depth = w_in.shape[0]
    assert seq % BLOCK == 0 and dec_batch % SAMPLE_DB == 0 and dec_seq <= POOL_HIST

    w_in_b = w_in.astype(BF16)
    w_pool_b = w_pool_map.astype(BF16)
    w_out_b = w_out.astype(BF16)
    w_fg_b, w_fu_b, w_fd_b = (w.astype(BF16)[:, None] for w in (w_ffn_gate, w_ffn_up, w_ffn_down))
    w_eg_b, w_eu_b, w_ed_b = (w.astype(BF16) for w in (w_exp_gate, w_exp_up, w_exp_down))
    w_r = jnp.pad(w_router.astype(F32), ((0, 0), (0, 0), (0, LANES - N_EXPERTS)))
    b_r = jnp.pad(b_router.astype(F32), ((0, 0), (0, LANES - N_EXPERTS)), constant_values=NEG)[:, None]
    zero_wr = jnp.zeros((D_MODEL, LANES), F32)
    zero_br = jnp.zeros((1, LANES), F32)

    pos_p = jnp.arange(seq, dtype=jnp.int32)
    pos_s = jnp.tile(PAST_LEN + jnp.arange(dec_seq, dtype=jnp.int32), dec_batch)
    cos_p, sin_p = _rope_tables(pos_p)
    cos_s, sin_s = _rope_tables(pos_s)
    cos_x, sin_x = _rope_tables(jnp.concatenate([jnp.tile(pos_p[seq - BLOCK:], batch), pos_s]))

    m_p, m_s, m_t = batch * seq, dec_batch * dec_seq, batch * BLOCK
    tm_in_p = _row_tile(seq, 512)
    tm_ffn_p = _row_tile(m_p, 512)

    xp = x_prompt.reshape(m_p, D_MODEL)
    xs = x_sample.reshape(m_s, D_MODEL)
    ck = cache_k.reshape(depth, dec_batch, WINDOW, KV_DIM)
    cv = cache_v.reshape(depth, dec_batch, WINDOW, KV_DIM)

    n_split = min(2, depth)
    xx = jnp.concatenate([x_prompt[:, seq - BLOCK:].reshape(m_t, D_MODEL), xs], axis=0)

    kp_l, vp_l, pp_l, ks_l, vs_l, ps_l = [], [], [], [], [], []
    for l in range(depth):
        g_mix = norm_mix[l][None]
        g_ffn = norm_ffn[l][None]
        pscale = pool_scale[l][None]
        sinks = attn_sinks[l]
        split = l < n_split

        q, k, v, u, ga, gp = _inproj(xp, g_mix, w_in_b[l], cos_p, sin_p, tm=tm_in_p)
        if split:
            qx, kx, vx, ux, gax, gpx = _inproj(xx, g_mix, w_in[l], cos_x, sin_x, tm=m_t + m_s)
            k, v, u = (_put_tail(a, b[:m_t], batch, seq) for a, b in ((k, kx), (v, vx), (u, ux)))
        xp = _mixer_prompt(q, k, v, u, ga, gp, xp, sinks, w_pool_b[l], pscale, w_out_b[l], batch=batch, seq=seq)
        if split:
            mix_t = _mixer_prompt_tail(qx[:m_t], k, v, u, gax[:m_t], gpx[:m_t], sinks, w_pool_map[l], pscale,
                                       batch=batch, seq=seq)
            mix_s, nk, nv, npool = _mixer_sample(qx[m_t:], kx[m_t:], vx[m_t:], ux[m_t:], gax[m_t:], gpx[m_t:],
                                                 ck[l], cv[l], state_pool[l], sinks, w_pool_map[l], pscale,
                                                 dec_seq=dec_seq)
            xx = _proj_residual(jnp.concatenate([mix_t, mix_s], axis=0), xx, w_out[l])
            xp = _put_tail(xp, xx[:m_t], batch, seq)
        else:
            qs, ks, vs, us, gas, gps = _inproj(xs, g_mix, w_in_b[l], cos_s, sin_s, tm=m_s)
            mix_s, nk, nv, npool = _mixer_sample(qs, ks, vs, us, gas, gps, ck[l], cv[l], state_pool[l], sinks,
                                                 w_pool_b[l], pscale, dec_seq=dec_seq)
            xs = _proj_residual(mix_s, xs, w_out_b[l])
        kp_l.append(k.reshape(batch, seq, N_KV_HEADS, HEAD_DIM)[:, seq - WINDOW:])
        vp_l.append(v.reshape(batch, seq, N_KV_HEADS, HEAD_DIM)[:, seq - WINDOW:])
        pp_l.append(u.reshape(batch, seq, POOL_DIM)[:, seq - POOL_HIST:])
        ks_l.append(nk.reshape(dec_batch, WINDOW, N_KV_HEADS, HEAD_DIM))
        vs_l.append(nv.reshape(dec_batch, WINDOW, N_KV_HEADS, HEAD_DIM))
        ps_l.append(npool)

        i = l // 2
        if l % 2 == 0:
            xp = _ffn(xp, g_ffn, zero_wr, zero_br, w_fg_b[i], w_fu_b[i], w_fd_b[i], moe=False, tm=tm_ffn_p)
            if split and l + 1 < n_split:
                xx = _ffn(xx, g_ffn, zero_wr, zero_br, w_ffn_gate[i][None], w_ffn_up[i][None],
                          w_ffn_down[i][None], moe=False, tm=m_t + m_s)
                xp = _put_tail(xp, xx[:m_t], batch, seq)
            else:
                if split:
                    xs = xx[m_t:]
                xs = _ffn(xs, g_ffn, zero_wr, zero_br, w_fg_b[i], w_fu_b[i], w_fd_b[i], moe=False, tm=m_s)
        else:
            if split:
                xs = xx[m_t:]
            xp = _moe_routed(xp, g_ffn, w_r[i], b_r[i], w_eg_b[i], w_eu_b[i], w_ed_b[i])
            xs = _ffn(xs, g_ffn, w_r[i], b_r[i], w_eg_b[i], w_eu_b[i], w_ed_b[i], moe=True, tm=m_s)

    g_fin = norm_final[None]
    y_p = _final_norm(xp, g_fin, tm=tm_ffn_p).reshape(batch, seq, D_MODEL)
    y_s = _final_norm(xs, g_fin, tm=m_s).reshape(dec_batch, dec_seq, D_MODEL)
    return (y_p, y_s, jnp.stack(kp_l), jnp.stack(vp_l), jnp.stack(pp_l),
            jnp.stack(ks_l), jnp.stack(vs_l), jnp.stack(ps_l))
```

```python
import functools

import jax
import jax.numpy as jnp
from jax import lax
from jax.experimental import pallas as pl
from jax.experimental.pallas import tpu as pltpu

D_MODEL = 2048
HEAD_DIM = 64
N_HEADS = 32
N_KV_HEADS = 4
GROUP = N_HEADS // N_KV_HEADS
KV_DIM = N_KV_HEADS * HEAD_DIM
WINDOW = 128
BLOCK = 128
ROPE_THETA = 10000.0
PAST_LEN = 16384
POOL_WINDOWS = (2, 4, 8, 16)
POOL_GROUP_DIM = 256
POOL_DIM = 1024
POOL_OUT_GROUP = 512
POOL_HIST = 15
N_EXPERTS = 8
EPS = 1e-5
NEG = -1e30

BF16 = jnp.bfloat16
F32 = jnp.float32

LANES = 128
VMEM_LIMIT = 52 * 1024 * 1024

IN_TN = 512
IN_TM = 1024
IN_Q_TILES = D_MODEL // IN_TN
IN_KV_TILE = IN_Q_TILES
IN_U_START = IN_KV_TILE + 1
IN_U_TILES = POOL_DIM // IN_TN
IN_GA_START = IN_U_START + IN_U_TILES
IN_G_TILES = D_MODEL // IN_TN
IN_GP_START = IN_GA_START + IN_G_TILES
IN_TILES = IN_GP_START + IN_G_TILES


def _sigmoid(x):
    return 1.0 / (1.0 + jnp.exp(-x))


def _split(a):
    hi = a.astype(BF16)
    lo = (a - hi.astype(F32)).astype(BF16)
    return hi, lo


def _dot(a, b):
    return jnp.dot(a, b, preferred_element_type=F32)


def _dot_nt(a, b):
    return lax.dot_general(a, b, (((1,), (1,)), ((), ())), preferred_element_type=F32)


def _mm(a_hi, a_lo, w):
    if w.dtype == BF16:
        return _dot(a_hi, w)
    w_hi, w_lo = _split(w)
    return _dot(a_hi, w_hi) + (_dot(a_lo, w_hi) + _dot(a_hi, w_lo))


def _qk(q, k_hi, k_lo):
    if q.dtype == BF16:
        return _dot_nt(q, k_hi)
    q_hi, q_lo = _split(q)
    return _dot_nt(q_hi, k_hi) + (_dot_nt(q_lo, k_hi) + _dot_nt(q_hi, k_lo))


def _pv(p, v_hi, v_lo):
    if v_lo is None:
        return _dot(p.astype(BF16), v_hi)
    p_hi, p_lo = _split(p)
    return _dot(p_hi, v_hi) + (_dot(p_lo, v_hi) + _dot(p_hi, v_lo))


def _kv_operands(t, precise):
    return _split(t) if precise else (t.astype(BF16), None)


def _cols(t, c):
    return None if t is None else t[:, c]


def _pool_map(pm, w_ref, gi):
    pm_hi, pm_lo = _split(pm) if w_ref.dtype == F32 else (pm.astype(BF16), None)
    return _mm(pm_hi, pm_lo, w_ref[gi])


def _rms_norm(x, g):
    ms = jnp.mean(x * x, axis=-1, keepdims=True)
    return x * lax.rsqrt(ms + EPS) * g


def _rope(t, cos, sin_signed):
    width = t.shape[1]
    lane = lax.broadcasted_iota(jnp.int32, t.shape, 1)
    first_half = (lane % HEAD_DIM) < (HEAD_DIM // 2)
    rot = jnp.where(first_half, pltpu.roll(t, width - HEAD_DIM // 2, 1), pltpu.roll(t, HEAD_DIM // 2, 1))
    reps = width // cos.shape[1]
    return t * jnp.tile(cos, (1, reps)) + rot * jnp.tile(sin_signed, (1, reps))


def _inproj_kernel(x_ref, g_ref, w_ref, cos_ref, sin_ref, q_ref, k_ref, v_ref, u_ref, ga_ref, gp_ref,
                   xhi_sc, xlo_sc):
    j = pl.program_id(1)
    precise = w_ref.dtype == F32

    @pl.when(j == 0)
    def _():
        xn = _rms_norm(x_ref[...], g_ref[...])
        if precise:
            xhi_sc[...], xlo_sc[...] = _split(xn)
        else:
            xhi_sc[...] = xn.astype(BF16)

    z = _mm(xhi_sc[...], xlo_sc[...] if precise else None, w_ref[...])

    @pl.when(j < IN_Q_TILES)
    def _():
        q_ref[...] = (_rope(z, cos_ref[...], sin_ref[...]) * (HEAD_DIM ** -0.5)).astype(q_ref.dtype)

    @pl.when(j == IN_KV_TILE)
    def _():
        k_ref[...] = _rope(z[:, :KV_DIM], cos_ref[...], sin_ref[...])
        v_ref[...] = z[:, KV_DIM:]

    @pl.when((j >= IN_U_START) & (j < IN_GA_START))
    def _():
        u_ref[...] = z

    @pl.when((j >= IN_GA_START) & (j < IN_GP_START))
    def _():
        ga_ref[...] = z

    @pl.when(j >= IN_GP_START)
    def _():
        gp_ref[...] = z


def _inproj(x, g, w, cos, sin, *, layer, tm):
    m = x.shape[0]
    pos_blocks = cos.shape[0] // tm
    grid = (m // tm, IN_TILES)

    def clip(j, lo, n):
        return jnp.clip(j - lo, 0, n - 1)

    lo_rows = tm if w.dtype == F32 else 16
    return pl.pallas_call(
        _inproj_kernel,
        grid=grid,
        in_specs=[
            pl.BlockSpec((tm, D_MODEL), lambda i, j: (i, 0), pipeline_mode=pl.Buffered(1)),
            pl.BlockSpec((1, D_MODEL), lambda i, j: (0, 0)),
            pl.BlockSpec((None, D_MODEL, IN_TN), lambda i, j: (layer, 0, j)),
            pl.BlockSpec((tm, LANES), lambda i, j: (i % pos_blocks, 0)),
            pl.BlockSpec((tm, LANES), lambda i, j: (i % pos_blocks, 0)),
        ],
        out_specs=[
            pl.BlockSpec((tm, IN_TN), lambda i, j: (i, clip(j, 0, IN_Q_TILES))),
            pl.BlockSpec((tm, KV_DIM), lambda i, j: (i, 0)),
            pl.BlockSpec((tm, KV_DIM), lambda i, j: (i, 0)),
            pl.BlockSpec((tm, IN_TN), lambda i, j: (i, clip(j, IN_U_START, IN_U_TILES))),
            pl.BlockSpec((tm, IN_TN), lambda i, j: (i, clip(j, IN_GA_START, IN_G_TILES))),
            pl.BlockSpec((tm, IN_TN), lambda i, j: (i, clip(j, IN_GP_START, IN_G_TILES))),
        ],
        out_shape=[
            jax.ShapeDtypeStruct((m, D_MODEL), w.dtype),
            jax.ShapeDtypeStruct((m, KV_DIM), F32),
            jax.ShapeDtypeStruct((m, KV_DIM), F32),
            jax.ShapeDtypeStruct((m, POOL_DIM), F32),
            jax.ShapeDtypeStruct((m, D_MODEL), F32),
            jax.ShapeDtypeStruct((m, D_MODEL), F32),
        ],
        scratch_shapes=[pltpu.VMEM((tm, D_MODEL), BF16), pltpu.VMEM((lo_rows, D_MODEL), BF16)],
        compiler_params=pltpu.CompilerParams(
            dimension_semantics=("parallel", "arbitrary"), vmem_limit_bytes=VMEM_LIMIT),
        name="inproj_split" if w.dtype == F32 else "inproj",
    )(x, g, w, cos, sin)


def _mixer_prompt_kernel(sinks_ref, q_ref, kc_ref, kp_ref, vc_ref, vp_ref, uc_ref, up_ref, ga_ref, gp_ref,
                         wpool_ref, pscale_ref, *rest, first_block):
    precise = q_ref.dtype == F32
    if precise:
        mix_ref, ext_sc = rest
    else:
        x_ref, wout_ref, o_ref, ext_sc, mix_ref = rest
    n = first_block + pl.program_id(1)
    not_first = n > 0

    k_hi, k_lo = _kv_operands(jnp.concatenate([kp_ref[...], kc_ref[...]], axis=0), precise)
    v_hi, v_lo = _kv_operands(jnp.concatenate([vp_ref[...], vc_ref[...]], axis=0), precise)
    qi = lax.broadcasted_iota(jnp.int32, (BLOCK, 2 * BLOCK), 0)
    kj = lax.broadcasted_iota(jnp.int32, (BLOCK, 2 * BLOCK), 1)
    dist = qi + BLOCK - kj
    mask = (dist >= 0) & (dist <= WINDOW) & ((kj >= BLOCK) | not_first)

    ext_sc[0:16, :] = jnp.where(not_first, up_ref[BLOCK - 16:BLOCK, :], 0.0)
    ext_sc[16:16 + BLOCK, :] = uc_ref[...]
    pos = n * BLOCK + lax.broadcasted_iota(jnp.int32, (BLOCK, 1), 0)

    for kh in range(N_KV_HEADS):
        w = POOL_WINDOWS[kh]
        pcs = slice(kh * POOL_GROUP_DIM, (kh + 1) * POOL_GROUP_DIM)
        cur = ext_sc[16:16 + BLOCK, pcs]
        acc = cur
        for d in range(1, w):
            acc = acc + ext_sc[16 - d:16 - d + BLOCK, pcs]
        cnt = jnp.minimum(pos + 1, w).astype(F32)
        pm = acc / cnt - cur
        ocs = slice(kh * POOL_OUT_GROUP, (kh + 1) * POOL_OUT_GROUP)
        pg = _pool_map(pm, wpool_ref, kh) * pscale_ref[:, ocs]

        hs = slice(kh * HEAD_DIM, (kh + 1) * HEAD_DIM)
        parts = []
        for g in range(GROUP):
            h = kh * GROUP + g
            s = _qk(q_ref[:, h * HEAD_DIM:(h + 1) * HEAD_DIM], k_hi[:, hs], _cols(k_lo, hs))
            s = jnp.where(mask, s, NEG)
            sink = sinks_ref[h]
            m = jnp.maximum(jnp.max(s, axis=-1, keepdims=True), sink)
            p = jnp.exp(s - m)
            denom = jnp.sum(p, axis=-1, keepdims=True) + jnp.exp(sink - m)
            parts.append(_pv(p, v_hi[:, hs], _cols(v_lo, hs)) / denom)
        a = jnp.concatenate(parts, axis=1)
        mix = _sigmoid(ga_ref[:, ocs]) * a + _sigmoid(gp_ref[:, ocs]) * pg
        mix_ref[:, ocs] = mix.astype(mix_ref.dtype)

    if not precise:
        o_ref[...] = x_ref[...] + _dot(mix_ref[...], wout_ref[...])


def _mixer_prompt(q, k, v, u, ga, gp, x, sinks, wpool, pscale, wout, *, layer, batch, seq):
    nb = seq // BLOCK
    m = batch * seq

    def cur(b, n):
        return (b * nb + n, 0)

    def prev(b, n):
        return (jnp.maximum(b * nb + n - 1, 0), 0)

    return pl.pallas_call(
        functools.partial(_mixer_prompt_kernel, first_block=0),
        grid=(batch, nb),
        in_specs=[
            pl.BlockSpec(memory_space=pltpu.SMEM),
            pl.BlockSpec((BLOCK, D_MODEL), cur),
            pl.BlockSpec((BLOCK, KV_DIM), cur),
            pl.BlockSpec((BLOCK, KV_DIM), prev),
            pl.BlockSpec((BLOCK, KV_DIM), cur),
            pl.BlockSpec((BLOCK, KV_DIM), prev),
            pl.BlockSpec((BLOCK, POOL_DIM), cur),
            pl.BlockSpec((BLOCK, POOL_DIM), prev),
            pl.BlockSpec((BLOCK, D_MODEL), cur),
            pl.BlockSpec((BLOCK, D_MODEL), cur),
            pl.BlockSpec((None, N_KV_HEADS, POOL_GROUP_DIM, POOL_OUT_GROUP), lambda b, n: (layer, 0, 0, 0)),
            pl.BlockSpec((1, D_MODEL), lambda b, n: (0, 0)),
            pl.BlockSpec((BLOCK, D_MODEL), cur),
            pl.BlockSpec((None, D_MODEL, D_MODEL), lambda b, n: (layer, 0, 0)),
        ],
        out_specs=pl.BlockSpec((BLOCK, D_MODEL), cur),
        out_shape=jax.ShapeDtypeStruct((m, D_MODEL), F32),
        scratch_shapes=[pltpu.VMEM((16 + BLOCK, POOL_DIM), F32), pltpu.VMEM((BLOCK, D_MODEL), BF16)],
        compiler_params=pltpu.CompilerParams(
            dimension_semantics=("parallel", "arbitrary"), vmem_limit_bytes=VMEM_LIMIT),
        name="mixer_prompt",
    )(sinks, q, k, k, v, v, u, u, ga, gp, wpool, pscale, x, wout)


def _mixer_prompt_tail(q_t, k, v, u, ga_t, gp_t, sinks, wpool, pscale, *, layer, batch, seq):
    nb = seq // BLOCK

    def tail(b, n):
        return (b, 0)

    def cur(b, n):
        return (b * nb + nb - 1, 0)

    def prev(b, n):
        return (jnp.maximum(b * nb + nb - 2, 0), 0)

    return pl.pallas_call(
        functools.partial(_mixer_prompt_kernel, first_block=nb - 1),
        grid=(batch, 1),
        in_specs=[
            pl.BlockSpec(memory_space=pltpu.SMEM),
            pl.BlockSpec((BLOCK, D_MODEL), tail),
            pl.BlockSpec((BLOCK, KV_DIM), cur),
            pl.BlockSpec((BLOCK, KV_DIM), prev),
            pl.BlockSpec((BLOCK, KV_DIM), cur),
            pl.BlockSpec((BLOCK, KV_DIM), prev),
            pl.BlockSpec((BLOCK, POOL_DIM), cur),
            pl.BlockSpec((BLOCK, POOL_DIM), prev),
            pl.BlockSpec((BLOCK, D_MODEL), tail),
            pl.BlockSpec((BLOCK, D_MODEL), tail),
            pl.BlockSpec((None, N_KV_HEADS, POOL_GROUP_DIM, POOL_OUT_GROUP), lambda b, n: (layer, 0, 0, 0)),
            pl.BlockSpec((1, D_MODEL), lambda b, n: (0, 0)),
        ],
        out_specs=pl.BlockSpec((BLOCK, D_MODEL), tail),
        out_shape=jax.ShapeDtypeStruct((batch * BLOCK, D_MODEL), F32),
        scratch_shapes=[pltpu.VMEM((16 + BLOCK, POOL_DIM), F32)],
        compiler_params=pltpu.CompilerParams(
            dimension_semantics=("parallel", "arbitrary"), vmem_limit_bytes=VMEM_LIMIT),
        name="mixer_prompt_tail",
    )(sinks, q_t, k, k, v, v, u, u, ga_t, gp_t, wpool, pscale)


SAMPLE_DB = 8


def _mixer_sample_kernel(sinks_ref, q_ref, kn_ref, vn_ref, u_ref, ga_ref, gp_ref, ck_ref, cv_ref, pool_ref,
                         wpool_ref, pscale_ref, mix_ref, nk_ref, nv_ref, npool_ref, ext_sc, pm_sc, a_sc,
                         *, dec_seq):
    t_new = dec_seq
    rows = SAMPLE_DB * t_new
    precise = q_ref.dtype == F32
    kc_hi, kc_lo = _kv_operands(ck_ref[...].reshape(SAMPLE_DB * WINDOW, KV_DIM), precise)
    vc_hi, vc_lo = _kv_operands(cv_ref[...].reshape(SAMPLE_DB * WINDOW, KV_DIM), precise)
    kn_hi, kn_lo = _kv_operands(kn_ref[...], precise)
    vn_hi, vn_lo = _kv_operands(vn_ref[...], precise)

    rho = lax.broadcasted_iota(jnp.int32, (GROUP * rows, 1), 0)
    q_db = (rho % rows) // t_new
    q_t = rho % t_new
    q_g = rho // rows
    colc = lax.broadcasted_iota(jnp.int32, (1, SAMPLE_DB * WINDOW), 1)
    mask_c = (q_db == colc // WINDOW) & ((colc % WINDOW) >= q_t)
    coln = lax.broadcasted_iota(jnp.int32, (1, rows), 1)
    mask_n = (q_db == coln // t_new) & ((coln % t_new) <= q_t)

    for d in range(SAMPLE_DB):
        rs = slice(d * t_new, (d + 1) * t_new)
        ext_sc[d, 0:POOL_HIST, :] = pool_ref[d]
        ext_sc[d, POOL_HIST:POOL_HIST + t_new, :] = u_ref[rs, :]
        npool_ref[d] = ext_sc[d, t_new:t_new + POOL_HIST, :]
        nk_ref[d, 0:WINDOW - t_new, :] = ck_ref[d, t_new:WINDOW, :]
        nk_ref[d, WINDOW - t_new:WINDOW, :] = kn_ref[rs, :]
        nv_ref[d, 0:WINDOW - t_new, :] = cv_ref[d, t_new:WINDOW, :]
        nv_ref[d, WINDOW - t_new:WINDOW, :] = vn_ref[rs, :]
        for gi, w in enumerate(POOL_WINDOWS):
            pcs = slice(gi * POOL_GROUP_DIM, (gi + 1) * POOL_GROUP_DIM)
            cur = ext_sc[d, POOL_HIST:POOL_HIST + t_new, pcs]
            acc = cur
            for j in range(1, w):
                acc = acc + ext_sc[d, POOL_HIST - j:POOL_HIST - j + t_new, pcs]
            pm_sc[rs, pcs] = acc / float(w) - cur

    for kh in range(N_KV_HEADS):
        hs = slice(kh * HEAD_DIM, (kh + 1) * HEAD_DIM)
        qs = jnp.concatenate(
            [q_ref[:, (kh * GROUP + g) * HEAD_DIM:(kh * GROUP + g + 1) * HEAD_DIM] for g in range(GROUP)], axis=0)
        sc = jnp.where(mask_c, _qk(qs, kc_hi[:, hs], _cols(kc_lo, hs)), NEG)
        sn = jnp.where(mask_n, _qk(qs, kn_hi[:, hs], _cols(kn_lo, hs)), NEG)
        sink = jnp.zeros((GROUP * rows, 1), F32)
        for g in range(GROUP):
            sink = jnp.where(q_g == g, sinks_ref[kh * GROUP + g], sink)
        m = jnp.maximum(jnp.maximum(jnp.max(sc, axis=-1, keepdims=True), jnp.max(sn, axis=-1, keepdims=True)), sink)
        pc = jnp.exp(sc - m)
        pn = jnp.exp(sn - m)
        denom = jnp.sum(pc, axis=-1, keepdims=True) + jnp.sum(pn, axis=-1, keepdims=True) + jnp.exp(sink - m)
        o = (_pv(pc, vc_hi[:, hs], _cols(vc_lo, hs)) + _pv(pn, vn_hi[:, hs], _cols(vn_lo, hs))) / denom
        for g in range(GROUP):
            h = kh * GROUP + g
            a_sc[:, h * HEAD_DIM:(h + 1) * HEAD_DIM] = o[g * rows:(g + 1) * rows, :]

    for gi in range(N_KV_HEADS):
        pcs = slice(gi * POOL_GROUP_DIM, (gi + 1) * POOL_GROUP_DIM)
        ocs = slice(gi * POOL_OUT_GROUP, (gi + 1) * POOL_OUT_GROUP)
        pg = _pool_map(pm_sc[:, pcs], wpool_ref, gi) * pscale_ref[:, ocs]
        mix_ref[:, ocs] = _sigmoid(ga_ref[:, ocs]) * a_sc[:, ocs] + _sigmoid(gp_ref[:, ocs]) * pg


def _mixer_sample(q, kn, vn, u, ga, gp, ck, cv, pool, sinks, wpool, pscale, *, layer, dec_seq):
    db = ck.shape[1]
    rows = SAMPLE_DB * dec_seq
    m = db * dec_seq
    row_blk = lambda i: (i, 0)
    db_blk = lambda i: (i, 0, 0)
    db_layer_blk = lambda i: (layer, i, 0, 0)
    return pl.pallas_call(
        functools.partial(_mixer_sample_kernel, dec_seq=dec_seq),
        grid=(db // SAMPLE_DB,),
        in_specs=[
            pl.BlockSpec(memory_space=pltpu.SMEM),
            pl.BlockSpec((rows, D_MODEL), row_blk),
            pl.BlockSpec((rows, KV_DIM), row_blk),
            pl.BlockSpec((rows, KV_DIM), row_blk),
            pl.BlockSpec((rows, POOL_DIM), row_blk),
            pl.BlockSpec((rows, D_MODEL), row_blk),
            pl.BlockSpec((rows, D_MODEL), row_blk),
            pl.BlockSpec((None, SAMPLE_DB, WINDOW, KV_DIM), db_layer_blk),
            pl.BlockSpec((None, SAMPLE_DB, WINDOW, KV_DIM), db_layer_blk),
            pl.BlockSpec((None, SAMPLE_DB, POOL_HIST, POOL_DIM), db_layer_blk),
            pl.BlockSpec((None, N_KV_HEADS, POOL_GROUP_DIM, POOL_OUT_GROUP), lambda i: (layer, 0, 0, 0)),
            pl.BlockSpec((1, D_MODEL), lambda i: (0, 0)),
        ],
        out_specs=[
            pl.BlockSpec((rows, D_MODEL), row_blk),
            pl.BlockSpec((SAMPLE_DB, WINDOW, KV_DIM), db_blk),
            pl.BlockSpec((SAMPLE_DB, WINDOW, KV_DIM), db_blk),
            pl.BlockSpec((SAMPLE_DB, POOL_HIST, POOL_DIM), db_blk),
        ],
        out_shape=[
            jax.ShapeDtypeStruct((m, D_MODEL), F32),
            jax.ShapeDtypeStruct((db, WINDOW, KV_DIM), F32),
            jax.ShapeDtypeStruct((db, WINDOW, KV_DIM), F32),
            jax.ShapeDtypeStruct((db, POOL_HIST, POOL_DIM), F32),
        ],
        scratch_shapes=[
            pltpu.VMEM((SAMPLE_DB, 24, POOL_DIM), F32),
            pltpu.VMEM((rows, POOL_DIM), F32),
            pltpu.VMEM((rows, D_MODEL), F32),
        ],
        compiler_params=pltpu.CompilerParams(
            dimension_semantics=("arbitrary",), vmem_limit_bytes=VMEM_LIMIT),
        name="mixer_sample_split" if q.dtype == F32 else "mixer_sample",
    )(sinks, q, kn, vn, u, ga, gp, ck, cv, pool, wpool, pscale)


PROJ_TN = 512


def _proj_residual_kernel(a_ref, x_ref, w_ref, o_ref):
    a = a_ref[...]
    a_hi, a_lo = _split(a) if w_ref.dtype == F32 else (a.astype(BF16), None)
    o_ref[...] = x_ref[...] + _mm(a_hi, a_lo, w_ref[...])


def _proj_residual(a, x, w, *, layer):
    m = x.shape[0]
    return pl.pallas_call(
        _proj_residual_kernel,
        grid=(D_MODEL // PROJ_TN,),
        in_specs=[
            pl.BlockSpec((m, D_MODEL), lambda j: (0, 0)),
            pl.BlockSpec((m, PROJ_TN), lambda j: (0, j)),
            pl.BlockSpec((None, D_MODEL, PROJ_TN), lambda j: (layer, 0, j)),
        ],
        out_specs=pl.BlockSpec((m, PROJ_TN), lambda j: (0, j)),
        out_shape=jax.ShapeDtypeStruct((m, D_MODEL), F32),
        compiler_params=pltpu.CompilerParams(
            dimension_semantics=("parallel",), vmem_limit_bytes=VMEM_LIMIT),
        name="proj_residual_split" if w.dtype == F32 else "proj_residual",
    )(a, x, w)


FFN_TF = 512
FFN_TM = 1024


def _ffn_kernel(x_ref, g_ref, wr_ref, br_ref, wg_ref, wu_ref, wd_ref, o_ref, xhi_sc, xlo_sc, comb_sc, *, moe):
    e = pl.program_id(1)
    f = pl.program_id(2)
    precise = wg_ref.dtype == F32

    @pl.when((e == 0) & (f == 0))
    def _():
        x = x_ref[...]
        xn = _rms_norm(x, g_ref[...])
        o_ref[...] = x
        if precise or moe:
            xn_hi, xn_lo = _split(xn)
            xhi_sc[...] = xn_hi
            if precise:
                xlo_sc[...] = xn_lo
        else:
            xhi_sc[...] = xn.astype(BF16)
        if moe:
            logits = _mm(xn_hi, xn_lo, wr_ref[...]) + br_ref[...]
            lane = lax.broadcasted_iota(jnp.int32, logits.shape, 1)
            v1 = jnp.max(logits, axis=-1, keepdims=True)
            i1 = jnp.min(jnp.where(logits == v1, lane, LANES), axis=-1, keepdims=True)
            rest = jnp.where(lane == i1, NEG, logits)
            v2 = jnp.max(rest, axis=-1, keepdims=True)
            i2 = jnp.min(jnp.where(rest == v2, lane, LANES), axis=-1, keepdims=True)
            e2 = jnp.exp(v2 - v1)
            g1 = 1.0 / (1.0 + e2)
            g2 = e2 / (1.0 + e2)
            comb_sc[...] = jnp.where(lane == i1, g1, 0.0) + jnp.where(lane == i2, g2, 0.0)

    xn_hi = xhi_sc[...]
    xn_lo = xlo_sc[...] if precise else None
    gate = _mm(xn_hi, xn_lo, wg_ref[0])
    up = _mm(xn_hi, xn_lo, wu_ref[0])
    h = gate * _sigmoid(gate) * up
    if moe:
        lane = lax.broadcasted_iota(jnp.int32, comb_sc.shape, 1)
        h = h * jnp.sum(jnp.where(lane == e, comb_sc[...], 0.0), axis=-1, keepdims=True)
    h_hi, h_lo = _split(h) if precise else (h.astype(BF16), None)
    o_ref[...] += _mm(h_hi, h_lo, wd_ref[0])


def _ffn(x, g, wr, br, wg, wu, wd, *, layer, moe, tm):
    m = x.shape[0]
    _, n_e, _, d_f = wg.shape
    n_f = d_f // FFN_TF
    lo_rows = tm if wg.dtype == F32 else 16
    return pl.pallas_call(
        functools.partial(_ffn_kernel, moe=moe),
        grid=(m // tm, n_e, n_f),
        in_specs=[
            pl.BlockSpec((tm, D_MODEL), lambda i, e, f: (i, 0), pipeline_mode=pl.Buffered(1)),
            pl.BlockSpec((1, D_MODEL), lambda i, e, f: (0, 0)),
            pl.BlockSpec((None, D_MODEL, LANES), lambda i, e, f: (layer, 0, 0)),
            pl.BlockSpec((None, 1, LANES), lambda i, e, f: (layer, 0, 0)),
            pl.BlockSpec((None, 1, D_MODEL, FFN_TF), lambda i, e, f: (layer, e, 0, f)),
            pl.BlockSpec((None, 1, D_MODEL, FFN_TF), lambda i, e, f: (layer, e, 0, f)),
            pl.BlockSpec((None, 1, FFN_TF, D_MODEL), lambda i, e, f: (layer, e, f, 0)),
        ],
        out_specs=pl.BlockSpec((tm, D_MODEL), lambda i, e, f: (i, 0)),
        out_shape=jax.ShapeDtypeStruct((m, D_MODEL), F32),
        scratch_shapes=[pltpu.VMEM((tm, D_MODEL), BF16), pltpu.VMEM((lo_rows, D_MODEL), BF16),
                        pltpu.VMEM((tm, LANES), F32)],
        compiler_params=pltpu.CompilerParams(
            dimension_semantics=("parallel", "arbitrary", "arbitrary"), vmem_limit_bytes=VMEM_LIMIT),
        name=("ffn_moe" if moe else "ffn_dense") + ("_split" if wg.dtype == F32 else ""),
    )(x, g, wr, br, wg, wu, wd)


MOE_TM = 512
MOE_TF = 1024
MOE_ROUTE_TM = 512
MOE_MOVE_TM = 256


def _route_kernel(x_ref, g_ref, wr_ref, br_ref, o_ref):
    xn_hi, xn_lo = _split(_rms_norm(x_ref[...], g_ref[...]))
    logits = _mm(xn_hi, xn_lo, wr_ref[...]) + br_ref[...]
    lane = lax.broadcasted_iota(jnp.int32, logits.shape, 1)
    v1 = jnp.max(logits, axis=-1, keepdims=True)
    i1 = jnp.min(jnp.where(logits == v1, lane, LANES), axis=-1, keepdims=True)
    rest = jnp.where(lane == i1, NEG, logits)
    v2 = jnp.max(rest, axis=-1, keepdims=True)
    i2 = jnp.min(jnp.where(rest == v2, lane, LANES), axis=-1, keepdims=True)
    e2 = jnp.exp(v2 - v1)
    g1 = 1.0 / (1.0 + e2)
    g2 = e2 / (1.0 + e2)
    o_ref[...] = jnp.where(lane == 0, i1.astype(F32),
                           jnp.where(lane == 1, i2.astype(F32),
                                     jnp.where(lane == 2, g1, jnp.where(lane == 3, g2, 0.0))))


def _route(x, g, wr, br, *, layer):
    m = x.shape[0]
    tm = _row_tile(m, MOE_ROUTE_TM)
    return pl.pallas_call(
        _route_kernel,
        grid=(m // tm,),
        in_specs=[
            pl.BlockSpec((tm, D_MODEL), lambda i: (i, 0)),
            pl.BlockSpec((1, D_MODEL), lambda i: (0, 0)),
            pl.BlockSpec((None, D_MODEL, LANES), lambda i: (layer, 0, 0)),
            pl.BlockSpec((None, 1, LANES), lambda i: (layer, 0, 0)),
        ],
        out_specs=pl.BlockSpec((tm, LANES), lambda i: (i, 0)),
        out_shape=jax.ShapeDtypeStruct((m, LANES), F32),
        compiler_params=pltpu.CompilerParams(dimension_semantics=("parallel",), vmem_limit_bytes=VMEM_LIMIT),
        name="moe_route",
    )(x, g, wr, br)


def _moe_plan(route, n_tiles):
    experts = jnp.arange(N_EXPERTS, dtype=jnp.int32)
    e_pair = route[:, :2].astype(jnp.int32).reshape(-1)
    onehot = (e_pair[:, None] == experts[None, :]).astype(jnp.int32)
    csum = jnp.cumsum(onehot, axis=0)
    rank = jnp.sum(csum * onehot, axis=1) - 1
    tiles_e = (csum[-1] + MOE_TM - 1) // MOE_TM
    tend = jnp.cumsum(tiles_e)
    tstart = tend - tiles_e
    pos = (jnp.sum(onehot * tstart[None, :], axis=1) * MOE_TM + rank).astype(jnp.int32)
    n_used = tend[-1:].astype(jnp.int32)
    t_idx = jnp.arange(n_tiles, dtype=jnp.int32)
    tile_e = jnp.sum((t_idx[:, None] >= tend[None, :]).astype(jnp.int32), axis=1)
    last_e = jnp.max(jnp.where(tiles_e > 0, experts, 0))
    tile_e = jnp.minimum(tile_e, last_e).astype(jnp.int32)
    return pos, tile_e, n_used


def _row_copy(src, dst, sem):
    return pltpu.make_async_copy(src, dst, sem)


def _push_kernel(pos_ref, x_ref, g_ref, zeros_ref, xs_ref, xn_sc, sem):
    del zeros_ref
    tm = x_ref.shape[0]
    base = pl.program_id(0) * tm
    xn_sc[...] = _rms_norm(x_ref[...], g_ref[...])

    def start(r, carry):
        for c in range(2):
            dst = pos_ref[2 * (base + r) + c]
            _row_copy(xn_sc.at[pl.ds(r, 1)], xs_ref.at[pl.ds(dst, 1)], sem).start()
        return carry

    lax.fori_loop(0, tm, start, 0, unroll=8)

    def wait(r, carry):
        for c in range(2):
            _row_copy(xn_sc.at[pl.ds(r, 1)], xs_ref.at[pl.ds(0, 1)], sem).wait()
        return carry

    lax.fori_loop(0, tm, wait, 0, unroll=8)


def _push(x, g, pos, n_rows):
    m = x.shape[0]
    tm = _row_tile(m, MOE_MOVE_TM)
    zeros = jnp.zeros((n_rows, D_MODEL), F32)
    return pl.pallas_call(
        _push_kernel,
        grid_spec=pltpu.PrefetchScalarGridSpec(
            num_scalar_prefetch=1,
            grid=(m // tm,),
            in_specs=[
                pl.BlockSpec((tm, D_MODEL), lambda i, pos: (i, 0)),
                pl.BlockSpec((1, D_MODEL), lambda i, pos: (0, 0)),
                pl.BlockSpec(memory_space=pl.ANY),
            ],
            out_specs=pl.BlockSpec(memory_space=pl.ANY),
            scratch_shapes=[pltpu.VMEM((tm, D_MODEL), F32), pltpu.SemaphoreType.DMA],
        ),
        out_shape=jax.ShapeDtypeStruct((n_rows, D_MODEL), F32),
        input_output_aliases={3: 0},
        compiler_params=pltpu.CompilerParams(dimension_semantics=("arbitrary",), vmem_limit_bytes=VMEM_LIMIT),
        name="moe_push",
    )(pos, x, g, zeros)


def _experts_kernel(tile_e_ref, n_used_ref, x_ref, wg_ref, wu_ref, wd_ref, o_ref, xb_sc):
    t = pl.program_id(0)
    f = pl.program_id(1)
    used = t < n_used_ref[0]

    @pl.when(f == 0)
    def _():
        xb_sc[...] = x_ref[...].astype(BF16)
        o_ref[...] = jnp.zeros_like(o_ref)

    @pl.when(used)
    def _():
        xn = xb_sc[...]
        gate = _dot(xn, wg_ref[0])
        up = _dot(xn, wu_ref[0])
        h = gate * _sigmoid(gate) * up
        o_ref[...] += _dot(h.astype(BF16), wd_ref[0])


def _experts(xs, tile_e, n_used, wg, wu, wd, *, layer):
    n_rows = xs.shape[0]
    n_f = wg.shape[3] // MOE_TF
    return pl.pallas_call(
        _experts_kernel,
        grid_spec=pltpu.PrefetchScalarGridSpec(
            num_scalar_prefetch=2,
            grid=(n_rows // MOE_TM, n_f),
            in_specs=[
                pl.BlockSpec((MOE_TM, D_MODEL), lambda t, f, te, nu: (t, 0)),
                pl.BlockSpec((None, 1, D_MODEL, MOE_TF), lambda t, f, te, nu: (layer, te[t], 0, f)),
                pl.BlockSpec((None, 1, D_MODEL, MOE_TF), lambda t, f, te, nu: (layer, te[t], 0, f)),
                pl.BlockSpec((None, 1, MOE_TF, D_MODEL), lambda t, f, te, nu: (layer, te[t], f, 0)),
            ],
            out_specs=pl.BlockSpec((MOE_TM, D_MODEL), lambda t, f, te, nu: (t, 0)),
            scratch_shapes=[pltpu.VMEM((MOE_TM, D_MODEL), BF16)],
        ),
        out_shape=jax.ShapeDtypeStruct((n_rows, D_MODEL), F32),
        compiler_params=pltpu.CompilerParams(
            dimension_semantics=("arbitrary", "arbitrary"), vmem_limit_bytes=VMEM_LIMIT),
        name="moe_experts",
    )(tile_e, n_used, xs, wg, wu, wd)


def _combine_kernel(pos_ref, x_ref, route_ref, ys_ref, o_ref, y_sc, sem):
    tm = x_ref.shape[0]
    base = pl.program_id(0) * tm

    def start(r, carry):
        for c in range(2):
            src = pos_ref[2 * (base + r) + c]
            _row_copy(ys_ref.at[pl.ds(src, 1)], y_sc.at[c, pl.ds(r, 1)], sem).start()
        return carry

    lax.fori_loop(0, tm, start, 0, unroll=8)

    def wait(r, carry):
        for c in range(2):
            _row_copy(ys_ref.at[pl.ds(0, 1)], y_sc.at[c, pl.ds(r, 1)], sem).wait()
        return carry

    lax.fori_loop(0, tm, wait, 0, unroll=8)
    o_ref[...] = x_ref[...] + route_ref[:, 2:3] * y_sc[0] + route_ref[:, 3:4] * y_sc[1]


def _combine(x, route, pos, ys):
    m = x.shape[0]
    tm = _row_tile(m, MOE_MOVE_TM)
    return pl.pallas_call(
        _combine_kernel,
        grid_spec=pltpu.PrefetchScalarGridSpec(
            num_scalar_prefetch=1,
            grid=(m // tm,),
            in_specs=[
                pl.BlockSpec((tm, D_MODEL), lambda i, pos: (i, 0)),
                pl.BlockSpec((tm, LANES), lambda i, pos: (i, 0)),
                pl.BlockSpec(memory_space=pl.ANY),
            ],
            out_specs=pl.BlockSpec((tm, D_MODEL), lambda i, pos: (i, 0)),
            scratch_shapes=[pltpu.VMEM((2, tm, D_MODEL), F32), pltpu.SemaphoreType.DMA],
        ),
        out_shape=jax.ShapeDtypeStruct((m, D_MODEL), F32),
        compiler_params=pltpu.CompilerParams(dimension_semantics=("arbitrary",), vmem_limit_bytes=VMEM_LIMIT),
        name="moe_combine",
    )(pos, x, route, ys)


def _moe_routed(x, g, wr, br, wg, wu, wd, *, layer):
    m = x.shape[0]
    n_tiles = (2 * m) // MOE_TM + N_EXPERTS
    route = _route(x, g, wr, br, layer=layer)
    pos, tile_e, n_used = _moe_plan(route, n_tiles)
    xs = _push(x, g, pos, n_tiles * MOE_TM)
    ys = _experts(xs, tile_e, n_used, wg, wu, wd, layer=layer)
    return _combine(x, route, pos, ys)


def _final_norm_kernel(x_ref, g_ref, o_ref):
    o_ref[...] = _rms_norm(x_ref[...], g_ref[...])


def _final_norm(x, g, *, tm):
    m = x.shape[0]
    return pl.pallas_call(
        _final_norm_kernel,
        grid=(m // tm,),
        in_specs=[pl.BlockSpec((tm, D_MODEL), lambda i: (i, 0)), pl.BlockSpec((1, D_MODEL), lambda i: (0, 0))],
        out_specs=pl.BlockSpec((tm, D_MODEL), lambda i: (i, 0)),
        out_shape=jax.ShapeDtypeStruct((m, D_MODEL), F32),
        compiler_params=pltpu.CompilerParams(dimension_semantics=("parallel",), vmem_limit_bytes=VMEM_LIMIT),
        name="final_norm",
    )(x, g)


def _rope_tables(pos):
    half = HEAD_DIM // 2
    inv = ROPE_THETA ** (-2.0 * jnp.arange(half, dtype=F32) / HEAD_DIM)
    ang = pos.astype(F32)[:, None] * inv[None, :]
    cos, sin = jnp.cos(ang), jnp.sin(ang)
    reps = LANES // HEAD_DIM
    return (jnp.tile(jnp.concatenate([cos, cos], axis=-1), (1, reps)),
            jnp.tile(jnp.concatenate([-sin, sin], axis=-1), (1, reps)))


def _row_tile(m, cap):
    t = cap
    while m % t:
        t //= 2
    return t


def _put_tail(main, tail, batch, seq):
    c = main.shape[1]
    upd = tail.reshape(batch, BLOCK, c).astype(main.dtype)
    return lax.dynamic_update_slice(main.reshape(batch, seq, c), upd, (0, seq - BLOCK, 0)).reshape(batch * seq, c)


def kernel(x_prompt, x_sample, cache_k, cache_v, state_pool, norm_mix, norm_ffn, norm_final, w_in, attn_sinks,
           w_pool_map, pool_scale, w_out, w_ffn_gate, w_ffn_up, w_ffn_down, w_router, b_router, w_exp_gate,
           w_exp_up, w_exp_down):
    batch, seq, _ = x_prompt.shape
    dec_batch, dec_seq, _ = x_sample.shape
    depth = w_in.shape[0]
    assert seq % BLOCK == 0 and dec_batch % SAMPLE_DB == 0 and dec_seq <= POOL_HIST

    w_in_b = w_in.astype(BF16)
    w_pool_b = w_pool_map.astype(BF16)
    w_out_b = w_out.astype(BF16)
    w_fg, w_fu, w_fd = (w[:, None] for w in (w_ffn_gate, w_ffn_up, w_ffn_down))
    w_fg_b, w_fu_b, w_fd_b = (w.astype(BF16) for w in (w_fg, w_fu, w_fd))
    w_eg_b, w_eu_b, w_ed_b = (w.astype(BF16) for w in (w_exp_gate, w_exp_up, w_exp_down))
    w_r = jnp.pad(w_router.astype(F32), ((0, 0), (0, 0), (0, LANES - N_EXPERTS)))
    b_r = jnp.pad(b_router.astype(F32), ((0, 0), (0, LANES - N_EXPERTS)), constant_values=NEG)[:, None]
    zero_wr = jnp.zeros((w_fg.shape[0], D_MODEL, LANES), F32)
    zero_br = jnp.zeros((w_fg.shape[0], 1, LANES), F32)

    pos_p = jnp.arange(seq, dtype=jnp.int32)
    pos_s = jnp.tile(PAST_LEN + jnp.arange(dec_seq, dtype=jnp.int32), dec_batch)
    cos_p, sin_p = _rope_tables(pos_p)
    cos_s, sin_s = _rope_tables(pos_s)
    cos_x, sin_x = _rope_tables(jnp.concatenate([jnp.tile(pos_p[seq - BLOCK:], batch), pos_s]))

    m_p, m_s, m_t = batch * seq, dec_batch * dec_seq, batch * BLOCK
    tm_in_p = _row_tile(seq, IN_TM)
    tm_ffn_p = _row_tile(m_p, FFN_TM)

    xp = x_prompt.reshape(m_p, D_MODEL)
    xs = x_sample.reshape(m_s, D_MODEL)
    ck = cache_k.reshape(depth, dec_batch, WINDOW, KV_DIM)
    cv = cache_v.reshape(depth, dec_batch, WINDOW, KV_DIM)

    n_split = min(2, depth)
    xx = jnp.concatenate([x_prompt[:, seq - BLOCK:].reshape(m_t, D_MODEL), xs], axis=0)

    kp_l, vp_l, pp_l, ks_l, vs_l, ps_l = [], [], [], [], [], []
    for l in range(depth):
        g_mix = norm_mix[l][None]
        g_ffn = norm_ffn[l][None]
        pscale = pool_scale[l][None]
        sinks = attn_sinks[l]
        split = l < n_split

        q, k, v, u, ga, gp = _inproj(xp, g_mix, w_in_b, cos_p, sin_p, layer=l, tm=tm_in_p)
        if split:
            qx, kx, vx, ux, gax, gpx = _inproj(xx, g_mix, w_in, cos_x, sin_x, layer=l, tm=m_t + m_s)
            k, v, u = (_put_tail(a, b[:m_t], batch, seq) for a, b in ((k, kx), (v, vx), (u, ux)))
        xp = _mixer_prompt(q, k, v, u, ga, gp, xp, sinks, w_pool_b, pscale, w_out_b, layer=l, batch=batch, seq=seq)
        if split:
            mix_t = _mixer_prompt_tail(qx[:m_t], k, v, u, gax[:m_t], gpx[:m_t], sinks, w_pool_map, pscale,
                                       layer=l, batch=batch, seq=seq)
            mix_s, nk, nv, npool = _mixer_sample(qx[m_t:], kx[m_t:], vx[m_t:], ux[m_t:], gax[m_t:], gpx[m_t:],
                                                 ck, cv, state_pool, sinks, w_pool_map, pscale,
                                                 layer=l, dec_seq=dec_seq)
            xx = _proj_residual(jnp.concatenate([mix_t, mix_s], axis=0), xx, w_out, layer=l)
            xp = _put_tail(xp, xx[:m_t], batch, seq)
        else:
            qs, ks, vs, us, gas, gps = _inproj(xs, g_mix, w_in_b, cos_s, sin_s, layer=l, tm=m_s)
            mix_s, nk, nv, npool = _mixer_sample(qs, ks, vs, us, gas, gps, ck, cv, state_pool, sinks,
                                                 w_pool_b, pscale, layer=l, dec_seq=dec_seq)
            xs = _proj_residual(mix_s, xs, w_out_b, layer=l)
        kp_l.append(k.reshape(batch, seq, N_KV_HEADS, HEAD_DIM)[:, seq - WINDOW:])
        vp_l.append(v.reshape(batch, seq, N_KV_HEADS, HEAD_DIM)[:, seq - WINDOW:])
        pp_l.append(u.reshape(batch, seq, POOL_DIM)[:, seq - POOL_HIST:])
        ks_l.append(nk.reshape(dec_batch, WINDOW, N_KV_HEADS, HEAD_DIM))
        vs_l.append(nv.reshape(dec_batch, WINDOW, N_KV_HEADS, HEAD_DIM))
        ps_l.append(npool)

        i = l // 2
        if l % 2 == 0:
            xp = _ffn(xp, g_ffn, zero_wr, zero_br, w_fg_b, w_fu_b, w_fd_b, layer=i, moe=False, tm=tm_ffn_p)
            if split and l + 1 < n_split:
                xx = _ffn(xx, g_ffn, zero_wr, zero_br, w_fg, w_fu, w_fd, layer=i, moe=False, tm=m_t + m_s)
                xp = _put_tail(xp, xx[:m_t], batch, seq)
            else:
                if split:
                    xs = xx[m_t:]
                xs = _ffn(xs, g_ffn, zero_wr, zero_br, w_fg_b, w_fu_b, w_fd_b, layer=i, moe=False, tm=m_s)
        else:
            if split:
                xs = xx[m_t:]
            xp = _moe_routed(xp, g_ffn, w_r, b_r, w_eg_b, w_eu_b, w_ed_b, layer=i)
            xs = _ffn(xs, g_ffn, w_r, b_r, w_eg_b, w_eu_b, w_ed_b, layer=i, moe=True, tm=m_s)

    g_fin = norm_final[None]
    y_p = _final_norm(xp, g_fin, tm=tm_ffn_p).reshape(batch, seq, D_MODEL)
    y_s = _final_norm(xs, g_fin, tm=m_s).reshape(dec_batch, dec_seq, D_MODEL)
    return (y_p, y_s, jnp.stack(kp_l), jnp.stack(vp_l), jnp.stack(pp_l),
            jnp.stack(ks_l), jnp.stack(vs_l), jnp.stack(ps_l))
```

```python
import functools

import jax
import jax.numpy as jnp
from jax import lax
from jax.experimental import pallas as pl
from jax.experimental.pallas import tpu as pltpu

D_MODEL = 2048
HEAD_DIM = 64
N_HEADS = 32
N_KV_HEADS = 4
GROUP = N_HEADS // N_KV_HEADS
KV_DIM = N_KV_HEADS * HEAD_DIM
WINDOW = 128
BLOCK = 128
ROPE_THETA = 10000.0
PAST_LEN = 16384
POOL_WINDOWS = (2, 4, 8, 16)
POOL_GROUP_DIM = 256
POOL_DIM = 1024
POOL_OUT_GROUP = 512
POOL_HIST = 15
N_EXPERTS = 8
EPS = 1e-5
NEG = -1e30

BF16 = jnp.bfloat16
F32 = jnp.float32

LANES = 128
VMEM_LIMIT = 52 * 1024 * 1024

IN_TN = 512
IN_TM = 1024
IN_Q_TILES = D_MODEL // IN_TN
IN_KV_TILE = IN_Q_TILES
IN_U_START = IN_KV_TILE + 1
IN_U_TILES = POOL_DIM // IN_TN
IN_GA_START = IN_U_START + IN_U_TILES
IN_G_TILES = D_MODEL // IN_TN
IN_GP_START = IN_GA_START + IN_G_TILES
IN_TILES = IN_GP_START + IN_G_TILES


def _sigmoid(x):
    return 1.0 / (1.0 + jnp.exp(-x))


def _split(a):
    hi = a.astype(BF16)
    lo = (a - hi.astype(F32)).astype(BF16)
    return hi, lo


def _dot(a, b):
    return jnp.dot(a, b, preferred_element_type=F32)


def _dot_nt(a, b):
    return lax.dot_general(a, b, (((1,), (1,)), ((), ())), preferred_element_type=F32)


def _mm(a_hi, a_lo, w):
    if w.dtype == BF16:
        return _dot(a_hi, w)
    w_hi, w_lo = _split(w)
    return _dot(a_hi, w_hi) + (_dot(a_lo, w_hi) + _dot(a_hi, w_lo))


def _qk(q, k_hi, k_lo):
    if q.dtype == BF16:
        return _dot_nt(q, k_hi)
    q_hi, q_lo = _split(q)
    return _dot_nt(q_hi, k_hi) + (_dot_nt(q_lo, k_hi) + _dot_nt(q_hi, k_lo))


def _pv(p, v_hi, v_lo):
    if v_lo is None:
        return _dot(p.astype(BF16), v_hi)
    p_hi, p_lo = _split(p)
    return _dot(p_hi, v_hi) + (_dot(p_lo, v_hi) + _dot(p_hi, v_lo))


def _kv_operands(t, precise):
    return _split(t) if precise else (t.astype(BF16), None)


def _cols(t, c):
    return None if t is None else t[:, c]


def _pool_map(pm, w_ref, gi):
    pm_hi, pm_lo = _split(pm) if w_ref.dtype == F32 else (pm.astype(BF16), None)
    return _mm(pm_hi, pm_lo, w_ref[gi])


def _rms_norm(x, g):
    ms = jnp.mean(x * x, axis=-1, keepdims=True)
    return x * lax.rsqrt(ms + EPS) * g


def _rope(t, cos, sin_signed):
    width = t.shape[1]
    lane = lax.broadcasted_iota(jnp.int32, t.shape, 1)
    first_half = (lane % HEAD_DIM) < (HEAD_DIM // 2)
    rot = jnp.where(first_half, pltpu.roll(t, width - HEAD_DIM // 2, 1), pltpu.roll(t, HEAD_DIM // 2, 1))
    reps = width // cos.shape[1]
    return t * jnp.tile(cos, (1, reps)) + rot * jnp.tile(sin_signed, (1, reps))


def _inproj_kernel(x_ref, g_ref, w_ref, cos_ref, sin_ref, q_ref, k_ref, v_ref, u_ref, ga_ref, gp_ref,
                   xhi_sc, xlo_sc):
    j = pl.program_id(1)
    precise = w_ref.dtype == F32

    @pl.when(j == 0)
    def _():
        xn = _rms_norm(x_ref[...], g_ref[...])
        if precise:
            xhi_sc[...], xlo_sc[...] = _split(xn)
        else:
            xhi_sc[...] = xn.astype(BF16)

    def z():
        return _mm(xhi_sc[...], xlo_sc[...] if precise else None, w_ref[...])

    @pl.when(j < IN_Q_TILES)
    def _():
        q_ref[...] = (_rope(z(), cos_ref[...], sin_ref[...]) * (HEAD_DIM ** -0.5)).astype(q_ref.dtype)

    @pl.when(j == IN_KV_TILE)
    def _():
        kv = z()
        k_ref[...] = _rope(kv[:, :KV_DIM], cos_ref[...], sin_ref[...])
        v_ref[...] = kv[:, KV_DIM:]

    @pl.when((j >= IN_U_START) & (j < IN_GA_START))
    def _():
        u_ref[...] = z()

    @pl.when((j >= IN_GA_START) & (j < IN_GP_START))
    def _():
        ga_ref[...] = z()

    @pl.when(j >= IN_GP_START)
    def _():
        gp_ref[...] = z()


def _inproj(x, g, w, cos, sin, *, layer, tm):
    m = x.shape[0]
    pos_blocks = cos.shape[0] // tm
    grid = (m // tm, IN_TILES)

    def clip(j, lo, n):
        return jnp.clip(j - lo, 0, n - 1)

    lo_rows = tm if w.dtype == F32 else 16
    return pl.pallas_call(
        _inproj_kernel,
        grid=grid,
        in_specs=[
            pl.BlockSpec((tm, D_MODEL), lambda i, j: (i, 0), pipeline_mode=pl.Buffered(1)),
            pl.BlockSpec((1, D_MODEL), lambda i, j: (0, 0)),
            pl.BlockSpec((None, D_MODEL, IN_TN), lambda i, j: (layer, 0, j)),
            pl.BlockSpec((tm, LANES), lambda i, j: (i % pos_blocks, 0)),
            pl.BlockSpec((tm, LANES), lambda i, j: (i % pos_blocks, 0)),
        ],
        out_specs=[
            pl.BlockSpec((tm, IN_TN), lambda i, j: (i, clip(j, 0, IN_Q_TILES))),
            pl.BlockSpec((tm, KV_DIM), lambda i, j: (i, 0)),
            pl.BlockSpec((tm, KV_DIM), lambda i, j: (i, 0)),
            pl.BlockSpec((tm, IN_TN), lambda i, j: (i, clip(j, IN_U_START, IN_U_TILES))),
            pl.BlockSpec((tm, IN_TN), lambda i, j: (i, clip(j, IN_GA_START, IN_G_TILES))),
            pl.BlockSpec((tm, IN_TN), lambda i, j: (i, clip(j, IN_GP_START, IN_G_TILES))),
        ],
        out_shape=[
            jax.ShapeDtypeStruct((m, D_MODEL), w.dtype),
            jax.ShapeDtypeStruct((m, KV_DIM), F32),
            jax.ShapeDtypeStruct((m, KV_DIM), F32),
            jax.ShapeDtypeStruct((m, POOL_DIM), F32),
            jax.ShapeDtypeStruct((m, D_MODEL), F32),
            jax.ShapeDtypeStruct((m, D_MODEL), F32),
        ],
        scratch_shapes=[pltpu.VMEM((tm, D_MODEL), BF16), pltpu.VMEM((lo_rows, D_MODEL), BF16)],
        compiler_params=pltpu.CompilerParams(
            dimension_semantics=("parallel", "arbitrary"), vmem_limit_bytes=VMEM_LIMIT),
        name="inproj_split" if w.dtype == F32 else "inproj",
    )(x, g, w, cos, sin)


def _mixer_prompt_kernel(sinks_ref, q_ref, kc_ref, kp_ref, vc_ref, vp_ref, uc_ref, up_ref, ga_ref, gp_ref,
                         wpool_ref, pscale_ref, *rest, first_block):
    precise = q_ref.dtype == F32
    if precise:
        mix_ref, ext_sc = rest
    else:
        x_ref, wout_ref, o_ref, ext_sc, mix_ref = rest
    n = first_block + pl.program_id(1)
    not_first = n > 0

    kb = jnp.concatenate([kp_ref[...], kc_ref[...]], axis=0)
    vb = jnp.concatenate([vp_ref[...], vc_ref[...]], axis=0)
    qi = lax.broadcasted_iota(jnp.int32, (BLOCK, 2 * BLOCK), 0)
    kj = lax.broadcasted_iota(jnp.int32, (BLOCK, 2 * BLOCK), 1)
    dist = qi + BLOCK - kj
    mask = (dist >= 0) & (dist <= WINDOW) & ((kj >= BLOCK) | not_first)
    lane = lax.broadcasted_iota(jnp.int32, (BLOCK, 2 * HEAD_DIM), 1)
    zeros_kv = jnp.zeros((2 * BLOCK, HEAD_DIM), F32)
    sum_cols = [(lax.broadcasted_iota(jnp.int32, (2 * BLOCK, 2 * HEAD_DIM), 1) == c).astype(F32) for c in range(2)]

    ext_sc[0:16, :] = jnp.where(not_first, up_ref[BLOCK - 16:BLOCK, :], 0.0)
    ext_sc[16:16 + BLOCK, :] = uc_ref[...]
    pos = n * BLOCK + lax.broadcasted_iota(jnp.int32, (BLOCK, 1), 0)

    for kh in range(N_KV_HEADS):
        w = POOL_WINDOWS[kh]
        pcs = slice(kh * POOL_GROUP_DIM, (kh + 1) * POOL_GROUP_DIM)
        cur = ext_sc[16:16 + BLOCK, pcs]
        acc = cur
        for d in range(1, w):
            acc = acc + ext_sc[16 - d:16 - d + BLOCK, pcs]
        inv_cnt = 1.0 / jnp.minimum(pos + 1, w).astype(F32)
        pm = acc * inv_cnt - cur
        ocs = slice(kh * POOL_OUT_GROUP, (kh + 1) * POOL_OUT_GROUP)
        pg = _pool_map(pm, wpool_ref, kh) * pscale_ref[:, ocs]

        hs = slice(kh * HEAD_DIM, (kh + 1) * HEAD_DIM)
        k_h, v_h = kb[:, hs], vb[:, hs]
        k_ops = [_kv_operands(jnp.concatenate(c, axis=1), precise) for c in ([k_h, zeros_kv], [zeros_kv, k_h])]
        v_ops = [_kv_operands(jnp.concatenate(c, axis=1), precise)
                 for c in ([v_h, zeros_kv, sum_cols[0]], [zeros_kv, v_h, sum_cols[1]])]
        pairs = []
        for pair in range(GROUP // 2):
            slab = (kh * GROUP // 2 + pair) * 2 * HEAD_DIM
            q_pair = q_ref[:, slab:slab + 2 * HEAD_DIM]
            out = None
            inv_denoms = []
            for c in range(2):
                s = jnp.where(mask, _qk(q_pair, *k_ops[c]), NEG)
                sink = sinks_ref[kh * GROUP + 2 * pair + c]
                m = jnp.maximum(jnp.max(s, axis=-1, keepdims=True), sink)
                o = _pv(jnp.exp(s - m), *v_ops[c])
                out = o if out is None else out + o
                inv_denoms.append(jnp.exp(sink - m))
            a2 = out[:, :2 * HEAD_DIM]
            inv_denoms = [1.0 / (out[:, 2 * HEAD_DIM + c:2 * HEAD_DIM + c + 1] + inv_denoms[c]) for c in range(2)]
            pairs.append(a2 * jnp.where(lane < HEAD_DIM, inv_denoms[0], inv_denoms[1]))
        a = jnp.concatenate(pairs, axis=1)
        mix = _sigmoid(ga_ref[:, ocs]) * a + _sigmoid(gp_ref[:, ocs]) * pg
        mix_ref[:, ocs] = mix.astype(mix_ref.dtype)

    if not precise:
        for c in range(D_MODEL // PROJ_TN):
            cs = slice(c * PROJ_TN, (c + 1) * PROJ_TN)
            o_ref[:, cs] = x_ref[:, cs] + _dot(mix_ref[...], wout_ref[:, cs])


def _mixer_prompt(q, k, v, u, ga, gp, x, sinks, wpool, pscale, wout, *, layer, batch, seq):
    nb = seq // BLOCK
    m = batch * seq

    def cur(b, n):
        return (b * nb + n, 0)

    def prev(b, n):
        return (jnp.maximum(b * nb + n - 1, 0), 0)

    return pl.pallas_call(
        functools.partial(_mixer_prompt_kernel, first_block=0),
        grid=(batch, nb),
        in_specs=[
            pl.BlockSpec(memory_space=pltpu.SMEM),
            pl.BlockSpec((BLOCK, D_MODEL), cur),
            pl.BlockSpec((BLOCK, KV_DIM), cur),
            pl.BlockSpec((BLOCK, KV_DIM), prev),
            pl.BlockSpec((BLOCK, KV_DIM), cur),
            pl.BlockSpec((BLOCK, KV_DIM), prev),
            pl.BlockSpec((BLOCK, POOL_DIM), cur),
            pl.BlockSpec((BLOCK, POOL_DIM), prev),
            pl.BlockSpec((BLOCK, D_MODEL), cur),
            pl.BlockSpec((BLOCK, D_MODEL), cur),
            pl.BlockSpec((None, N_KV_HEADS, POOL_GROUP_DIM, POOL_OUT_GROUP), lambda b, n: (layer, 0, 0, 0)),
            pl.BlockSpec((1, D_MODEL), lambda b, n: (0, 0)),
            pl.BlockSpec((BLOCK, D_MODEL), cur),
            pl.BlockSpec((None, D_MODEL, D_MODEL), lambda b, n: (layer, 0, 0)),
        ],
        out_specs=pl.BlockSpec((BLOCK, D_MODEL), cur),
        out_shape=jax.ShapeDtypeStruct((m, D_MODEL), F32),
        scratch_shapes=[pltpu.VMEM((16 + BLOCK, POOL_DIM), F32), pltpu.VMEM((BLOCK, D_MODEL), BF16)],
        compiler_params=pltpu.CompilerParams(
            dimension_semantics=("parallel", "arbitrary"), vmem_limit_bytes=VMEM_LIMIT),
        name="mixer_prompt",
    )(sinks, q, k, k, v, v, u, u, ga, gp, wpool, pscale, x, wout)


def _mixer_prompt_tail(q_t, k, v, u, ga_t, gp_t, sinks, wpool, pscale, *, layer, batch, seq):
    nb = seq // BLOCK

    def tail(b, n):
        return (b, 0)

    def cur(b, n):
        return (b * nb + nb - 1, 0)

    def prev(b, n):
        return (jnp.maximum(b * nb + nb - 2, 0), 0)

    return pl.pallas_call(
        functools.partial(_mixer_prompt_kernel, first_block=nb - 1),
        grid=(batch, 1),
        in_specs=[
            pl.BlockSpec(memory_space=pltpu.SMEM),
            pl.BlockSpec((BLOCK, D_MODEL), tail),
            pl.BlockSpec((BLOCK, KV_DIM), cur),
            pl.BlockSpec((BLOCK, KV_DIM), prev),
            pl.BlockSpec((BLOCK, KV_DIM), cur),
            pl.BlockSpec((BLOCK, KV_DIM), prev),
            pl.BlockSpec((BLOCK, POOL_DIM), cur),
            pl.BlockSpec((BLOCK, POOL_DIM), prev),
            pl.BlockSpec((BLOCK, D_MODEL), tail),
            pl.BlockSpec((BLOCK, D_MODEL), tail),
            pl.BlockSpec((None, N_KV_HEADS, POOL_GROUP_DIM, POOL_OUT_GROUP), lambda b, n: (layer, 0, 0, 0)),
            pl.BlockSpec((1, D_MODEL), lambda b, n: (0, 0)),
        ],
        out_specs=pl.BlockSpec((BLOCK, D_MODEL), tail),
        out_shape=jax.ShapeDtypeStruct((batch * BLOCK, D_MODEL), F32),
        scratch_shapes=[pltpu.VMEM((16 + BLOCK, POOL_DIM), F32)],
        compiler_params=pltpu.CompilerParams(
            dimension_semantics=("parallel", "arbitrary"), vmem_limit_bytes=VMEM_LIMIT),
        name="mixer_prompt_tail",
    )(sinks, q_t, k, k, v, v, u, u, ga_t, gp_t, wpool, pscale)


SAMPLE_DB = 8


def _mixer_sample_kernel(sinks_ref, q_ref, kn_ref, vn_ref, u_ref, ga_ref, gp_ref, ck_ref, cv_ref, pool_ref,
                         wpool_ref, pscale_ref, mix_ref, nk_ref, nv_ref, npool_ref, ext_sc, pm_sc, a_sc,
                         *, dec_seq):
    t_new = dec_seq
    rows = SAMPLE_DB * t_new
    precise = q_ref.dtype == F32
    kc_hi, kc_lo = _kv_operands(ck_ref[...].reshape(SAMPLE_DB * WINDOW, KV_DIM), precise)
    vc_hi, vc_lo = _kv_operands(cv_ref[...].reshape(SAMPLE_DB * WINDOW, KV_DIM), precise)
    kn_hi, kn_lo = _kv_operands(kn_ref[...], precise)
    vn_hi, vn_lo = _kv_operands(vn_ref[...], precise)

    rho = lax.broadcasted_iota(jnp.int32, (GROUP * rows, 1), 0)
    q_db = (rho % rows) // t_new
    q_t = rho % t_new
    q_g = rho // rows
    colc = lax.broadcasted_iota(jnp.int32, (1, SAMPLE_DB * WINDOW), 1)
    mask_c = (q_db == colc // WINDOW) & ((colc % WINDOW) >= q_t)
    coln = lax.broadcasted_iota(jnp.int32, (1, rows), 1)
    mask_n = (q_db == coln // t_new) & ((coln % t_new) <= q_t)

    for d in range(SAMPLE_DB):
        rs = slice(d * t_new, (d + 1) * t_new)
        ext_sc[d, 0:POOL_HIST, :] = pool_ref[d]
        ext_sc[d, POOL_HIST:POOL_HIST + t_new, :] = u_ref[rs, :]
        npool_ref[d] = ext_sc[d, t_new:t_new + POOL_HIST, :]
        nk_ref[d, 0:WINDOW - t_new, :] = ck_ref[d, t_new:WINDOW, :]
        nk_ref[d, WINDOW - t_new:WINDOW, :] = kn_ref[rs, :]
        nv_ref[d, 0:WINDOW - t_new, :] = cv_ref[d, t_new:WINDOW, :]
        nv_ref[d, WINDOW - t_new:WINDOW, :] = vn_ref[rs, :]
        for gi, w in enumerate(POOL_WINDOWS):
            pcs = slice(gi * POOL_GROUP_DIM, (gi + 1) * POOL_GROUP_DIM)
            cur = ext_sc[d, POOL_HIST:POOL_HIST + t_new, pcs]
            acc = cur
            for j in range(1, w):
                acc = acc + ext_sc[d, POOL_HIST - j:POOL_HIST - j + t_new, pcs]
            pm_sc[rs, pcs] = acc / float(w) - cur

    for kh in range(N_KV_HEADS):
        hs = slice(kh * HEAD_DIM, (kh + 1) * HEAD_DIM)
        qs = jnp.concatenate(
            [q_ref[:, (kh * GROUP + g) * HEAD_DIM:(kh * GROUP + g + 1) * HEAD_DIM] for g in range(GROUP)], axis=0)
        sc = jnp.where(mask_c, _qk(qs, kc_hi[:, hs], _cols(kc_lo, hs)), NEG)
        sn = jnp.where(mask_n, _qk(qs, kn_hi[:, hs], _cols(kn_lo, hs)), NEG)
        sink = jnp.zeros((GROUP * rows, 1), F32)
        for g in range(GROUP):
            sink = jnp.where(q_g == g, sinks_ref[kh * GROUP + g], sink)
        m = jnp.maximum(jnp.maximum(jnp.max(sc, axis=-1, keepdims=True), jnp.max(sn, axis=-1, keepdims=True)), sink)
        pc = jnp.exp(sc - m)
        pn = jnp.exp(sn - m)
        denom = jnp.sum(pc, axis=-1, keepdims=True) + jnp.sum(pn, axis=-1, keepdims=True) + jnp.exp(sink - m)
        o = (_pv(pc, vc_hi[:, hs], _cols(vc_lo, hs)) + _pv(pn, vn_hi[:, hs], _cols(vn_lo, hs))) / denom
        for g in range(GROUP):
            h = kh * GROUP + g
            a_sc[:, h * HEAD_DIM:(h + 1) * HEAD_DIM] = o[g * rows:(g + 1) * rows, :]

    for gi in range(N_KV_HEADS):
        pcs = slice(gi * POOL_GROUP_DIM, (gi + 1) * POOL_GROUP_DIM)
        ocs = slice(gi * POOL_OUT_GROUP, (gi + 1) * POOL_OUT_GROUP)
        pg = _pool_map(pm_sc[:, pcs], wpool_ref, gi) * pscale_ref[:, ocs]
        mix_ref[:, ocs] = _sigmoid(ga_ref[:, ocs]) * a_sc[:, ocs] + _sigmoid(gp_ref[:, ocs]) * pg


def _mixer_sample(q, kn, vn, u, ga, gp, ck, cv, pool, sinks, wpool, pscale, *, layer, dec_seq):
    db = ck.shape[1]
    rows = SAMPLE_DB * dec_seq
    m = db * dec_seq
    row_blk = lambda i: (i, 0)
    db_blk = lambda i: (i, 0, 0)
    db_layer_blk = lambda i: (layer, i, 0, 0)
    return pl.pallas_call(
        functools.partial(_mixer_sample_kernel, dec_seq=dec_seq),
        grid=(db // SAMPLE_DB,),
        in_specs=[
            pl.BlockSpec(memory_space=pltpu.SMEM),
            pl.BlockSpec((rows, D_MODEL), row_blk),
            pl.BlockSpec((rows, KV_DIM), row_blk),
            pl.BlockSpec((rows, KV_DIM), row_blk),
            pl.BlockSpec((rows, POOL_DIM), row_blk),
            pl.BlockSpec((rows, D_MODEL), row_blk),
            pl.BlockSpec((rows, D_MODEL), row_blk),
            pl.BlockSpec((None, SAMPLE_DB, WINDOW, KV_DIM), db_layer_blk),
            pl.BlockSpec((None, SAMPLE_DB, WINDOW, KV_DIM), db_layer_blk),
            pl.BlockSpec((None, SAMPLE_DB, POOL_HIST, POOL_DIM), db_layer_blk),
            pl.BlockSpec((None, N_KV_HEADS, POOL_GROUP_DIM, POOL_OUT_GROUP), lambda i: (layer, 0, 0, 0)),
            pl.BlockSpec((1, D_MODEL), lambda i: (0, 0)),
        ],
        out_specs=[
            pl.BlockSpec((rows, D_MODEL), row_blk),
            pl.BlockSpec((SAMPLE_DB, WINDOW, KV_DIM), db_blk),
            pl.BlockSpec((SAMPLE_DB, WINDOW, KV_DIM), db_blk),
            pl.BlockSpec((SAMPLE_DB, POOL_HIST, POOL_DIM), db_blk),
        ],
        out_shape=[
            jax.ShapeDtypeStruct((m, D_MODEL), F32),
            jax.ShapeDtypeStruct((db, WINDOW, KV_DIM), F32),
            jax.ShapeDtypeStruct((db, WINDOW, KV_DIM), F32),
            jax.ShapeDtypeStruct((db, POOL_HIST, POOL_DIM), F32),
        ],
        scratch_shapes=[
            pltpu.VMEM((SAMPLE_DB, 24, POOL_DIM), F32),
            pltpu.VMEM((rows, POOL_DIM), F32),
            pltpu.VMEM((rows, D_MODEL), F32),
        ],
        compiler_params=pltpu.CompilerParams(
            dimension_semantics=("arbitrary",), vmem_limit_bytes=VMEM_LIMIT),
        name="mixer_sample_split" if q.dtype == F32 else "mixer_sample",
    )(sinks, q, kn, vn, u, ga, gp, ck, cv, pool, wpool, pscale)


PROJ_TN = 512


def _proj_residual_kernel(a_ref, x_ref, w_ref, o_ref):
    a = a_ref[...]
    a_hi, a_lo = _split(a) if w_ref.dtype == F32 else (a.astype(BF16), None)
    o_ref[...] = x_ref[...] + _mm(a_hi, a_lo, w_ref[...])


def _proj_residual(a, x, w, *, layer):
    m = x.shape[0]
    return pl.pallas_call(
        _proj_residual_kernel,
        grid=(D_MODEL // PROJ_TN,),
        in_specs=[
            pl.BlockSpec((m, D_MODEL), lambda j: (0, 0)),
            pl.BlockSpec((m, PROJ_TN), lambda j: (0, j)),
            pl.BlockSpec((None, D_MODEL, PROJ_TN), lambda j: (layer, 0, j)),
        ],
        out_specs=pl.BlockSpec((m, PROJ_TN), lambda j: (0, j)),
        out_shape=jax.ShapeDtypeStruct((m, D_MODEL), F32),
        compiler_params=pltpu.CompilerParams(
            dimension_semantics=("parallel",), vmem_limit_bytes=VMEM_LIMIT),
        name="proj_residual_split" if w.dtype == F32 else "proj_residual",
    )(a, x, w)


FFN_TF = 512
FFN_TM = 1024


def _ffn_kernel(x_ref, g_ref, wr_ref, br_ref, wg_ref, wu_ref, wd_ref, o_ref, xhi_sc, xlo_sc, comb_sc, *, moe):
    e = pl.program_id(1)
    f = pl.program_id(2)
    precise = wg_ref.dtype == F32

    @pl.when((e == 0) & (f == 0))
    def _():
        x = x_ref[...]
        xn = _rms_norm(x, g_ref[...])
        o_ref[...] = x
        if precise or moe:
            xn_hi, xn_lo = _split(xn)
            xhi_sc[...] = xn_hi
            if precise:
                xlo_sc[...] = xn_lo
        else:
            xhi_sc[...] = xn.astype(BF16)
        if moe:
            logits = _mm(xn_hi, xn_lo, wr_ref[...]) + br_ref[...]
            lane = lax.broadcasted_iota(jnp.int32, logits.shape, 1)
            v1 = jnp.max(logits, axis=-1, keepdims=True)
            i1 = jnp.min(jnp.where(logits == v1, lane, LANES), axis=-1, keepdims=True)
            rest = jnp.where(lane == i1, NEG, logits)
            v2 = jnp.max(rest, axis=-1, keepdims=True)
            i2 = jnp.min(jnp.where(rest == v2, lane, LANES), axis=-1, keepdims=True)
            e2 = jnp.exp(v2 - v1)
            g1 = 1.0 / (1.0 + e2)
            g2 = e2 / (1.0 + e2)
            comb_sc[...] = jnp.where(lane == i1, g1, 0.0) + jnp.where(lane == i2, g2, 0.0)

    xn_hi = xhi_sc[...]
    xn_lo = xlo_sc[...] if precise else None
    gate = _mm(xn_hi, xn_lo, wg_ref[0])
    up = _mm(xn_hi, xn_lo, wu_ref[0])
    h = gate * _sigmoid(gate) * up
    if moe:
        lane = lax.broadcasted_iota(jnp.int32, comb_sc.shape, 1)
        h = h * jnp.sum(jnp.where(lane == e, comb_sc[...], 0.0), axis=-1, keepdims=True)
    h_hi, h_lo = _split(h) if precise else (h.astype(BF16), None)
    o_ref[...] += _mm(h_hi, h_lo, wd_ref[0])


def _ffn(x, g, wr, br, wg, wu, wd, *, layer, moe, tm):
    m = x.shape[0]
    _, n_e, _, d_f = wg.shape
    n_f = d_f // FFN_TF
    lo_rows = tm if wg.dtype == F32 else 16
    return pl.pallas_call(
        functools.partial(_ffn_kernel, moe=moe),
        grid=(m // tm, n_e, n_f),
        in_specs=[
            pl.BlockSpec((tm, D_MODEL), lambda i, e, f: (i, 0), pipeline_mode=pl.Buffered(1)),
            pl.BlockSpec((1, D_MODEL), lambda i, e, f: (0, 0)),
            pl.BlockSpec((None, D_MODEL, LANES), lambda i, e, f: (layer, 0, 0)),
            pl.BlockSpec((None, 1, LANES), lambda i, e, f: (layer, 0, 0)),
            pl.BlockSpec((None, 1, D_MODEL, FFN_TF), lambda i, e, f: (layer, e, 0, f)),
            pl.BlockSpec((None, 1, D_MODEL, FFN_TF), lambda i, e, f: (layer, e, 0, f)),
            pl.BlockSpec((None, 1, FFN_TF, D_MODEL), lambda i, e, f: (layer, e, f, 0)),
        ],
        out_specs=pl.BlockSpec((tm, D_MODEL), lambda i, e, f: (i, 0)),
        out_shape=jax.ShapeDtypeStruct((m, D_MODEL), F32),
        scratch_shapes=[pltpu.VMEM((tm, D_MODEL), BF16), pltpu.VMEM((lo_rows, D_MODEL), BF16),
                        pltpu.VMEM((tm, LANES), F32)],
        compiler_params=pltpu.CompilerParams(
            dimension_semantics=("parallel", "arbitrary", "arbitrary"), vmem_limit_bytes=VMEM_LIMIT),
        name=("ffn_moe" if moe else "ffn_dense") + ("_split" if wg.dtype == F32 else ""),
    )(x, g, wr, br, wg, wu, wd)


MOE_TM = 512
MOE_TF = 1024
MOE_ROUTE_TM = 512
MOE_MOVE_TM = 256


def _route_kernel(x_ref, g_ref, wr_ref, br_ref, o_ref):
    xn_hi, xn_lo = _split(_rms_norm(x_ref[...], g_ref[...]))
    logits = _mm(xn_hi, xn_lo, wr_ref[...]) + br_ref[...]
    lane = lax.broadcasted_iota(jnp.int32, logits.shape, 1)
    v1 = jnp.max(logits, axis=-1, keepdims=True)
    i1 = jnp.min(jnp.where(logits == v1, lane, LANES), axis=-1, keepdims=True)
    rest = jnp.where(lane == i1, NEG, logits)
    v2 = jnp.max(rest, axis=-1, keepdims=True)
    i2 = jnp.min(jnp.where(rest == v2, lane, LANES), axis=-1, keepdims=True)
    e2 = jnp.exp(v2 - v1)
    g1 = 1.0 / (1.0 + e2)
    g2 = e2 / (1.0 + e2)
    o_ref[...] = jnp.where(lane == 0, i1.astype(F32),
                           jnp.where(lane == 1, i2.astype(F32),
                                     jnp.where(lane == 2, g1, jnp.where(lane == 3, g2, 0.0))))


def _route(x, g, wr, br, *, layer):
    m = x.shape[0]
    tm = _row_tile(m, MOE_ROUTE_TM)
    return pl.pallas_call(
        _route_kernel,
        grid=(m // tm,),
        in_specs=[
            pl.BlockSpec((tm, D_MODEL), lambda i: (i, 0)),
            pl.BlockSpec((1, D_MODEL), lambda i: (0, 0)),
            pl.BlockSpec((None, D_MODEL, LANES), lambda i: (layer, 0, 0)),
            pl.BlockSpec((None, 1, LANES), lambda i: (layer, 0, 0)),
        ],
        out_specs=pl.BlockSpec((tm, LANES), lambda i: (i, 0)),
        out_shape=jax.ShapeDtypeStruct((m, LANES), F32),
        compiler_params=pltpu.CompilerParams(dimension_semantics=("parallel",), vmem_limit_bytes=VMEM_LIMIT),
        name="moe_route",
    )(x, g, wr, br)


def _moe_plan(route, n_tiles):
    experts = jnp.arange(N_EXPERTS, dtype=jnp.int32)
    e_pair = route[:, :2].astype(jnp.int32).reshape(-1)
    onehot = (e_pair[:, None] == experts[None, :]).astype(jnp.int32)
    csum = jnp.cumsum(onehot, axis=0)
    rank = jnp.sum(csum * onehot, axis=1) - 1
    tiles_e = (csum[-1] + MOE_TM - 1) // MOE_TM
    tend = jnp.cumsum(tiles_e)
    tstart = tend - tiles_e
    pos = (jnp.sum(onehot * tstart[None, :], axis=1) * MOE_TM + rank).astype(jnp.int32)
    n_used = tend[-1:].astype(jnp.int32)
    t_idx = jnp.arange(n_tiles, dtype=jnp.int32)
    tile_e = jnp.sum((t_idx[:, None] >= tend[None, :]).astype(jnp.int32), axis=1)
    last_e = jnp.max(jnp.where(tiles_e > 0, experts, 0))
    tile_e = jnp.minimum(tile_e, last_e).astype(jnp.int32)
    return pos, tile_e, n_used


def _row_copy(src, dst, sem):
    return pltpu.make_async_copy(src, dst, sem)


def _push_kernel(pos_ref, x_ref, g_ref, zeros_ref, xs_ref, xn_sc, sem):
    del zeros_ref
    tm = x_ref.shape[0]
    base = pl.program_id(0) * tm
    xn_sc[...] = _rms_norm(x_ref[...], g_ref[...])

    def start(r, carry):
        for c in range(2):
            dst = pos_ref[2 * (base + r) + c]
            _row_copy(xn_sc.at[pl.ds(r, 1)], xs_ref.at[pl.ds(dst, 1)], sem).start()
        return carry

    lax.fori_loop(0, tm, start, 0, unroll=8)

    def wait(r, carry):
        for c in range(2):
            _row_copy(xn_sc.at[pl.ds(r, 1)], xs_ref.at[pl.ds(0, 1)], sem).wait()
        return carry

    lax.fori_loop(0, tm, wait, 0, unroll=8)


def _push(x, g, pos, n_rows):
    m = x.shape[0]
    tm = _row_tile(m, MOE_MOVE_TM)
    zeros = jnp.zeros((n_rows, D_MODEL), F32)
    return pl.pallas_call(
        _push_kernel,
        grid_spec=pltpu.PrefetchScalarGridSpec(
            num_scalar_prefetch=1,
            grid=(m // tm,),
            in_specs=[
                pl.BlockSpec((tm, D_MODEL), lambda i, pos: (i, 0)),
                pl.BlockSpec((1, D_MODEL), lambda i, pos: (0, 0)),
                pl.BlockSpec(memory_space=pl.ANY),
            ],
            out_specs=pl.BlockSpec(memory_space=pl.ANY),
            scratch_shapes=[pltpu.VMEM((tm, D_MODEL), F32), pltpu.SemaphoreType.DMA],
        ),
        out_shape=jax.ShapeDtypeStruct((n_rows, D_MODEL), F32),
        input_output_aliases={3: 0},
        compiler_params=pltpu.CompilerParams(dimension_semantics=("arbitrary",), vmem_limit_bytes=VMEM_LIMIT),
        name="moe_push",
    )(pos, x, g, zeros)


def _experts_kernel(tile_e_ref, n_used_ref, x_ref, wg_ref, wu_ref, wd_ref, o_ref, xb_sc):
    t = pl.program_id(0)
    f = pl.program_id(1)
    used = t < n_used_ref[0]

    @pl.when(f == 0)
    def _():
        xb_sc[...] = x_ref[...].astype(BF16)
        o_ref[...] = jnp.zeros_like(o_ref)

    @pl.when(used)
    def _():
        xn = xb_sc[...]
        gate = _dot(xn, wg_ref[0])
        up = _dot(xn, wu_ref[0])
        h = gate * _sigmoid(gate) * up
        o_ref[...] += _dot(h.astype(BF16), wd_ref[0])


def _experts(xs, tile_e, n_used, wg, wu, wd, *, layer):
    n_rows = xs.shape[0]
    n_f = wg.shape[3] // MOE_TF
    return pl.pallas_call(
        _experts_kernel,
        grid_spec=pltpu.PrefetchScalarGridSpec(
            num_scalar_prefetch=2,
            grid=(n_rows // MOE_TM, n_f),
            in_specs=[
                pl.BlockSpec((MOE_TM, D_MODEL), lambda t, f, te, nu: (t, 0)),
                pl.BlockSpec((None, 1, D_MODEL, MOE_TF), lambda t, f, te, nu: (layer, te[t], 0, f)),
                pl.BlockSpec((None, 1, D_MODEL, MOE_TF), lambda t, f, te, nu: (layer, te[t], 0, f)),
                pl.BlockSpec((None, 1, MOE_TF, D_MODEL), lambda t, f, te, nu: (layer, te[t], f, 0)),
            ],
            out_specs=pl.BlockSpec((MOE_TM, D_MODEL), lambda t, f, te, nu: (t, 0)),
            scratch_shapes=[pltpu.VMEM((MOE_TM, D_MODEL), BF16)],
        ),
        out_shape=jax.ShapeDtypeStruct((n_rows, D_MODEL), F32),
        compiler_params=pltpu.CompilerParams(
            dimension_semantics=("arbitrary", "arbitrary"), vmem_limit_bytes=VMEM_LIMIT),
        name="moe_experts",
    )(tile_e, n_used, xs, wg, wu, wd)


def _combine_kernel(pos_ref, x_ref, route_ref, ys_ref, o_ref, y_sc, sem):
    tm = x_ref.shape[0]
    base = pl.program_id(0) * tm

    def start(r, carry):
        for c in range(2):
            src = pos_ref[2 * (base + r) + c]
            _row_copy(ys_ref.at[pl.ds(src, 1)], y_sc.at[c, pl.ds(r, 1)], sem).start()
        return carry

    lax.fori_loop(0, tm, start, 0, unroll=8)

    def wait(r, carry):
        for c in range(2):
            _row_copy(ys_ref.at[pl.ds(0, 1)], y_sc.at[c, pl.ds(r, 1)], sem).wait()
        return carry

    lax.fori_loop(0, tm, wait, 0, unroll=8)
    o_ref[...] = x_ref[...] + route_ref[:, 2:3] * y_sc[0] + route_ref[:, 3:4] * y_sc[1]


def _combine(x, route, pos, ys):
    m = x.shape[0]
    tm = _row_tile(m, MOE_MOVE_TM)
    return pl.pallas_call(
        _combine_kernel,
        grid_spec=pltpu.PrefetchScalarGridSpec(
            num_scalar_prefetch=1,
            grid=(m // tm,),
            in_specs=[
                pl.BlockSpec((tm, D_MODEL), lambda i, pos: (i, 0)),
                pl.BlockSpec((tm, LANES), lambda i, pos: (i, 0)),
                pl.BlockSpec(memory_space=pl.ANY),
            ],
            out_specs=pl.BlockSpec((tm, D_MODEL), lambda i, pos: (i, 0)),
            scratch_shapes=[pltpu.VMEM((2, tm, D_MODEL), F32), pltpu.SemaphoreType.DMA],
        ),
        out_shape=jax.ShapeDtypeStruct((m, D_MODEL), F32),
        compiler_params=pltpu.CompilerParams(dimension_semantics=("arbitrary",), vmem_limit_bytes=VMEM_LIMIT),
        name="moe_combine",
    )(pos, x, route, ys)


def _moe_routed(x, g, wr, br, wg, wu, wd, *, layer):
    m = x.shape[0]
    n_tiles = (2 * m) // MOE_TM + N_EXPERTS
    route = _route(x, g, wr, br, layer=layer)
    pos, tile_e, n_used = _moe_plan(route, n_tiles)
    xs = _push(x, g, pos, n_tiles * MOE_TM)
    ys = _experts(xs, tile_e, n_used, wg, wu, wd, layer=layer)
    return _combine(x, route, pos, ys)


def _final_norm_kernel(x_ref, g_ref, o_ref):
    o_ref[...] = _rms_norm(x_ref[...], g_ref[...])


def _final_norm(x, g, *, tm):
    m = x.shape[0]
    return pl.pallas_call(
        _final_norm_kernel,
        grid=(m // tm,),
        in_specs=[pl.BlockSpec((tm, D_MODEL), lambda i: (i, 0)), pl.BlockSpec((1, D_MODEL), lambda i: (0, 0))],
        out_specs=pl.BlockSpec((tm, D_MODEL), lambda i: (i, 0)),
        out_shape=jax.ShapeDtypeStruct((m, D_MODEL), F32),
        compiler_params=pltpu.CompilerParams(dimension_semantics=("parallel",), vmem_limit_bytes=VMEM_LIMIT),
        name="final_norm",
    )(x, g)


def _rope_tables(pos):
    half = HEAD_DIM // 2
    inv = ROPE_THETA ** (-2.0 * jnp.arange(half, dtype=F32) / HEAD_DIM)
    ang = pos.astype(F32)[:, None] * inv[None, :]
    cos, sin = jnp.cos(ang), jnp.sin(ang)
    reps = LANES // HEAD_DIM
    return (jnp.tile(jnp.concatenate([cos, cos], axis=-1), (1, reps)),
            jnp.tile(jnp.concatenate([-sin, sin], axis=-1), (1, reps)))


def _row_tile(m, cap):
    t = cap
    while m % t:
        t //= 2
    return t


def _put_tail(main, tail, batch, seq):
    c = main.shape[1]
    upd = tail.reshape(batch, BLOCK, c).astype(main.dtype)
    return lax.dynamic_update_slice(main.reshape(batch, seq, c), upd, (0, seq - BLOCK, 0)).reshape(batch * seq, c)


def kernel(x_prompt, x_sample, cache_k, cache_v, state_pool, norm_mix, norm_ffn, norm_final, w_in, attn_sinks,
           w_pool_map, pool_scale, w_out, w_ffn_gate, w_ffn_up, w_ffn_down, w_router, b_router, w_exp_gate,
           w_exp_up, w_exp_down):
    batch, seq, _ = x_prompt.shape
    dec_batch, dec_seq, _ = x_sample.shape
    depth = w_in.shape[0]
    assert seq % BLOCK == 0 and dec_batch % SAMPLE_DB == 0 and dec_seq <= POOL_HIST

    w_in_b = w_in.astype(BF16)
    w_pool_b = w_pool_map.astype(BF16)
    w_out_b = w_out.astype(BF16)
    w_fg, w_fu, w_fd = (w[:, None] for w in (w_ffn_gate, w_ffn_up, w_ffn_down))
    w_fg_b, w_fu_b, w_fd_b = (w.astype(BF16) for w in (w_fg, w_fu, w_fd))
    w_eg_b, w_eu_b, w_ed_b = (w.astype(BF16) for w in (w_exp_gate, w_exp_up, w_exp_down))
    w_r = jnp.pad(w_router.astype(F32), ((0, 0), (0, 0), (0, LANES - N_EXPERTS)))
    b_r = jnp.pad(b_router.astype(F32), ((0, 0), (0, LANES - N_EXPERTS)), constant_values=NEG)[:, None]
    zero_wr = jnp.zeros((w_fg.shape[0], D_MODEL, LANES), F32)
    zero_br = jnp.zeros((w_fg.shape[0], 1, LANES), F32)

    pos_p = jnp.arange(seq, dtype=jnp.int32)
    pos_s = jnp.tile(PAST_LEN + jnp.arange(dec_seq, dtype=jnp.int32), dec_batch)
    cos_p, sin_p = _rope_tables(pos_p)
    cos_s, sin_s = _rope_tables(pos_s)
    cos_x, sin_x = _rope_tables(jnp.concatenate([jnp.tile(pos_p[seq - BLOCK:], batch), pos_s]))

    m_p, m_s, m_t = batch * seq, dec_batch * dec_seq, batch * BLOCK
    tm_in_p = _row_tile(seq, IN_TM)
    tm_ffn_p = _row_tile(m_p, FFN_TM)

    xp = x_prompt.reshape(m_p, D_MODEL)
    xs = x_sample.reshape(m_s, D_MODEL)
    ck = cache_k.reshape(depth, dec_batch, WINDOW, KV_DIM)
    cv = cache_v.reshape(depth, dec_batch, WINDOW, KV_DIM)

    n_split = min(2, depth)
    xx = jnp.concatenate([x_prompt[:, seq - BLOCK:].reshape(m_t, D_MODEL), xs], axis=0)

    kp_l, vp_l, pp_l, ks_l, vs_l, ps_l = [], [], [], [], [], []
    for l in range(depth):
        g_mix = norm_mix[l][None]
        g_ffn = norm_ffn[l][None]
        pscale = pool_scale[l][None]
        sinks = attn_sinks[l]
        split = l < n_split

        q, k, v, u, ga, gp = _inproj(xp, g_mix, w_in_b, cos_p, sin_p, layer=l, tm=tm_in_p)
        if split:
            qx, kx, vx, ux, gax, gpx = _inproj(xx, g_mix, w_in, cos_x, sin_x, layer=l, tm=m_t + m_s)
            k, v, u = (_put_tail(a, b[:m_t], batch, seq) for a, b in ((k, kx), (v, vx), (u, ux)))
        xp = _mixer_prompt(q, k, v, u, ga, gp, xp, sinks, w_pool_b, pscale, w_out_b, layer=l, batch=batch, seq=seq)
        if split:
            mix_t = _mixer_prompt_tail(qx[:m_t], k, v, u, gax[:m_t], gpx[:m_t], sinks, w_pool_map, pscale,
                                       layer=l, batch=batch, seq=seq)
            mix_s, nk, nv, npool = _mixer_sample(qx[m_t:], kx[m_t:], vx[m_t:], ux[m_t:], gax[m_t:], gpx[m_t:],
                                                 ck, cv, state_pool, sinks, w_pool_map, pscale,
                                                 layer=l, dec_seq=dec_seq)
            xx = _proj_residual(jnp.concatenate([mix_t, mix_s], axis=0), xx, w_out, layer=l)
            xp = _put_tail(xp, xx[:m_t], batch, seq)
        else:
            qs, ks, vs, us, gas, gps = _inproj(xs, g_mix, w_in_b, cos_s, sin_s, layer=l, tm=m_s)
            mix_s, nk, nv, npool = _mixer_sample(qs, ks, vs, us, gas, gps, ck, cv, state_pool, sinks,
                                                 w_pool_b, pscale, layer=l, dec_seq=dec_seq)
            xs = _proj_residual(mix_s, xs, w_out_b, layer=l)
        kp_l.append(k.reshape(batch, seq, N_KV_HEADS, HEAD_DIM)[:, seq - WINDOW:])
        vp_l.append(v.reshape(batch, seq, N_KV_HEADS, HEAD_DIM)[:, seq - WINDOW:])
        pp_l.append(u.reshape(batch, seq, POOL_DIM)[:, seq - POOL_HIST:])
        ks_l.append(nk.reshape(dec_batch, WINDOW, N_KV_HEADS, HEAD_DIM))
        vs_l.append(nv.reshape(dec_batch, WINDOW, N_KV_HEADS, HEAD_DIM))
        ps_l.append(npool)

        i = l // 2
        if l % 2 == 0:
            xp = _ffn(xp, g_ffn, zero_wr, zero_br, w_fg_b, w_fu_b, w_fd_b, layer=i, moe=False, tm=tm_ffn_p)
            if split and l + 1 < n_split:
                xx = _ffn(xx, g_ffn, zero_wr, zero_br, w_fg, w_fu, w_fd, layer=i, moe=False, tm=m_t + m_s)
                xp = _put_tail(xp, xx[:m_t], batch, seq)
            else:
                if split:
                    xs = xx[m_t:]
                xs = _ffn(xs, g_ffn, zero_wr, zero_br, w_fg_b, w_fu_b, w_fd_b, layer=i, moe=False, tm=m_s)
        else:
            if split:
                xs = xx[m_t:]
            xp = _moe_routed(xp, g_ffn, w_r, b_r, w_eg_b, w_eu_b, w_ed_b, layer=i)
            xs = _ffn(xs, g_ffn, w_r, b_r, w_eg_b, w_eu_b, w_ed_b, layer=i, moe=True, tm=m_s)

    g_fin = norm_final[None]
    y_p = _final_norm(xp, g_fin, tm=tm_ffn_p).reshape(batch, seq, D_MODEL)
    y_s = _final_norm(xs, g_fin, tm=m_s).reshape(dec_batch, dec_seq, D_MODEL)
    return (y_p, y_s, jnp.stack(kp_l), jnp.stack(vp_l), jnp.stack(pp_l),
            jnp.stack(ks_l), jnp.stack(vs_l), jnp.stack(ps_l))
```

```python
import functools

import jax
import jax.numpy as jnp
from jax import lax
from jax.experimental import pallas as pl
from jax.experimental.pallas import tpu as pltpu

D_MODEL = 2048
HEAD_DIM = 64
N_HEADS = 32
N_KV_HEADS = 4
GROUP = N_HEADS // N_KV_HEADS
KV_DIM = N_KV_HEADS * HEAD_DIM
WINDOW = 128
BLOCK = 128
ROPE_THETA = 10000.0
PAST_LEN = 16384
POOL_WINDOWS = (2, 4, 8, 16)
POOL_GROUP_DIM = 256
POOL_DIM = 1024
POOL_OUT_GROUP = 512
POOL_HIST = 15
N_EXPERTS = 8
EPS = 1e-5
NEG = -1e30

BF16 = jnp.bfloat16
F32 = jnp.float32

LANES = 128
VMEM_LIMIT = 52 * 1024 * 1024

IN_TN = 512
IN_TM = 1024
IN_Q_TILES = D_MODEL // IN_TN
IN_KV_TILE = IN_Q_TILES
IN_U_START = IN_KV_TILE + 1
IN_U_TILES = POOL_DIM // IN_TN
IN_GA_START = IN_U_START + IN_U_TILES
IN_G_TILES = D_MODEL // IN_TN
IN_GP_START = IN_GA_START + IN_G_TILES
IN_TILES = IN_GP_START + IN_G_TILES


def _sigmoid(x):
    return 1.0 / (1.0 + jnp.exp(-x))


def _split(a):
    hi = a.astype(BF16)
    lo = (a - hi.astype(F32)).astype(BF16)
    return hi, lo


def _dot(a, b):
    return jnp.dot(a, b, preferred_element_type=F32)


def _dot_nt(a, b):
    return lax.dot_general(a, b, (((1,), (1,)), ((), ())), preferred_element_type=F32)


def _mm(a_hi, a_lo, w):
    if w.dtype == BF16:
        return _dot(a_hi, w)
    w_hi, w_lo = _split(w)
    return _dot(a_hi, w_hi) + (_dot(a_lo, w_hi) + _dot(a_hi, w_lo))


def _qk(q, k_hi, k_lo):
    if q.dtype == BF16:
        return _dot_nt(q, k_hi)
    q_hi, q_lo = _split(q)
    return _dot_nt(q_hi, k_hi) + (_dot_nt(q_lo, k_hi) + _dot_nt(q_hi, k_lo))


def _pv(p, v_hi, v_lo):
    if v_lo is None:
        return _dot(p.astype(BF16), v_hi)
    p_hi, p_lo = _split(p)
    return _dot(p_hi, v_hi) + (_dot(p_lo, v_hi) + _dot(p_hi, v_lo))


def _kv_operands(t, precise):
    return _split(t) if precise else (t.astype(BF16), None)


def _cols(t, c):
    return None if t is None else t[:, c]


def _pool_map(pm, w_ref, gi):
    pm_hi, pm_lo = _split(pm) if w_ref.dtype == F32 else (pm.astype(BF16), None)
    return _mm(pm_hi, pm_lo, w_ref[gi])


def _rms_norm(x, g):
    ms = jnp.mean(x * x, axis=-1, keepdims=True)
    return x * lax.rsqrt(ms + EPS) * g


def _rope(t, cos, sin_signed):
    width = t.shape[1]
    lane = lax.broadcasted_iota(jnp.int32, t.shape, 1)
    first_half = (lane % HEAD_DIM) < (HEAD_DIM // 2)
    rot = jnp.where(first_half, pltpu.roll(t, width - HEAD_DIM // 2, 1), pltpu.roll(t, HEAD_DIM // 2, 1))
    reps = width // cos.shape[1]
    return t * jnp.tile(cos, (1, reps)) + rot * jnp.tile(sin_signed, (1, reps))


def _inproj_kernel(x_ref, g_ref, w_ref, cos_ref, sin_ref, q_ref, k_ref, v_ref, u_ref, ga_ref, gp_ref,
                   xhi_sc, xlo_sc):
    j = pl.program_id(1)
    precise = w_ref.dtype == F32

    @pl.when(j == 0)
    def _():
        xn = _rms_norm(x_ref[...], g_ref[...])
        if precise:
            xhi_sc[...], xlo_sc[...] = _split(xn)
        else:
            xhi_sc[...] = xn.astype(BF16)

    def z():
        return _mm(xhi_sc[...], xlo_sc[...] if precise else None, w_ref[...])

    @pl.when(j < IN_Q_TILES)
    def _():
        q_ref[...] = (_rope(z(), cos_ref[...], sin_ref[...]) * (HEAD_DIM ** -0.5)).astype(q_ref.dtype)

    @pl.when(j == IN_KV_TILE)
    def _():
        kv = z()
        k_ref[...] = _rope(kv[:, :KV_DIM], cos_ref[...], sin_ref[...])
        v_ref[...] = kv[:, KV_DIM:]

    @pl.when((j >= IN_U_START) & (j < IN_GA_START))
    def _():
        u_ref[...] = z()

    @pl.when((j >= IN_GA_START) & (j < IN_GP_START))
    def _():
        ga_ref[...] = z()

    @pl.when(j >= IN_GP_START)
    def _():
        gp_ref[...] = z()


def _inproj(x, g, w, cos, sin, *, layer, tm):
    m = x.shape[0]
    pos_blocks = cos.shape[0] // tm
    grid = (m // tm, IN_TILES)

    def clip(j, lo, n):
        return jnp.clip(j - lo, 0, n - 1)

    lo_rows = tm if w.dtype == F32 else 16
    return pl.pallas_call(
        _inproj_kernel,
        grid=grid,
        in_specs=[
            pl.BlockSpec((tm, D_MODEL), lambda i, j: (i, 0), pipeline_mode=pl.Buffered(1)),
            pl.BlockSpec((1, D_MODEL), lambda i, j: (0, 0)),
            pl.BlockSpec((None, D_MODEL, IN_TN), lambda i, j: (layer, 0, j)),
            pl.BlockSpec((tm, LANES), lambda i, j: (i % pos_blocks, 0)),
            pl.BlockSpec((tm, LANES), lambda i, j: (i % pos_blocks, 0)),
        ],
        out_specs=[
            pl.BlockSpec((tm, IN_TN), lambda i, j: (i, clip(j, 0, IN_Q_TILES))),
            pl.BlockSpec((tm, KV_DIM), lambda i, j: (i, 0)),
            pl.BlockSpec((tm, KV_DIM), lambda i, j: (i, 0)),
            pl.BlockSpec((tm, IN_TN), lambda i, j: (i, clip(j, IN_U_START, IN_U_TILES))),
            pl.BlockSpec((tm, IN_TN), lambda i, j: (i, clip(j, IN_GA_START, IN_G_TILES))),
            pl.BlockSpec((tm, IN_TN), lambda i, j: (i, clip(j, IN_GP_START, IN_G_TILES))),
        ],
        out_shape=[
            jax.ShapeDtypeStruct((m, D_MODEL), w.dtype),
            jax.ShapeDtypeStruct((m, KV_DIM), F32),
            jax.ShapeDtypeStruct((m, KV_DIM), F32),
            jax.ShapeDtypeStruct((m, POOL_DIM), F32),
            jax.ShapeDtypeStruct((m, D_MODEL), F32),
            jax.ShapeDtypeStruct((m, D_MODEL), F32),
        ],
        scratch_shapes=[pltpu.VMEM((tm, D_MODEL), BF16), pltpu.VMEM((lo_rows, D_MODEL), BF16)],
        compiler_params=pltpu.CompilerParams(
            dimension_semantics=("parallel", "arbitrary"), vmem_limit_bytes=VMEM_LIMIT),
        name="inproj_split" if w.dtype == F32 else "inproj",
    )(x, g, w, cos, sin)


def _mixer_prompt_kernel(sinks_ref, q_ref, kc_ref, kp_ref, vc_ref, vp_ref, uc_ref, up_ref, ga_ref, gp_ref,
                         wpool_ref, pscale_ref, *rest, first_block, n_blocks):
    precise = q_ref.dtype == F32
    if precise:
        mix_ref, ext_sc = rest
        n = first_block + pl.program_id(1)
    else:
        x_ref, wout_ref, o_ref, ext_sc, mix_ref, prev_sc = rest
        step = pl.program_id(1)
        n = jnp.minimum(step, n_blocks - 1)

        @pl.when((pl.program_id(0) == 0) & (step == 0))
        def _():
            mix_ref[...] = jnp.zeros_like(mix_ref)

        prev_sc[...] = mix_ref[...]
    not_first = n > 0

    kb = jnp.concatenate([kp_ref[...], kc_ref[...]], axis=0)
    vb = jnp.concatenate([vp_ref[...], vc_ref[...]], axis=0)
    qi = lax.broadcasted_iota(jnp.int32, (BLOCK, 2 * BLOCK), 0)
    kj = lax.broadcasted_iota(jnp.int32, (BLOCK, 2 * BLOCK), 1)
    dist = qi + BLOCK - kj
    mask = (dist >= 0) & (dist <= WINDOW) & ((kj >= BLOCK) | not_first)
    lane = lax.broadcasted_iota(jnp.int32, (BLOCK, 2 * HEAD_DIM), 1)
    zeros_kv = jnp.zeros((2 * BLOCK, HEAD_DIM), F32)
    sum_cols = [(lax.broadcasted_iota(jnp.int32, (2 * BLOCK, 2 * HEAD_DIM), 1) == c).astype(F32) for c in range(2)]

    ext_sc[0:16, :] = jnp.where(not_first, up_ref[BLOCK - 16:BLOCK, :], 0.0)
    ext_sc[16:16 + BLOCK, :] = uc_ref[...]
    pos = n * BLOCK + lax.broadcasted_iota(jnp.int32, (BLOCK, 1), 0)

    for kh in range(N_KV_HEADS):
        w = POOL_WINDOWS[kh]
        pcs = slice(kh * POOL_GROUP_DIM, (kh + 1) * POOL_GROUP_DIM)
        cur = ext_sc[16:16 + BLOCK, pcs]
        acc = cur
        for d in range(1, w):
            acc = acc + ext_sc[16 - d:16 - d + BLOCK, pcs]
        inv_cnt = 1.0 / jnp.minimum(pos + 1, w).astype(F32)
        pm = acc * inv_cnt - cur
        ocs = slice(kh * POOL_OUT_GROUP, (kh + 1) * POOL_OUT_GROUP)
        pg = _pool_map(pm, wpool_ref, kh) * pscale_ref[:, ocs]

        hs = slice(kh * HEAD_DIM, (kh + 1) * HEAD_DIM)
        k_h, v_h = kb[:, hs], vb[:, hs]
        k_ops = [_kv_operands(jnp.concatenate(c, axis=1), precise) for c in ([k_h, zeros_kv], [zeros_kv, k_h])]
        v_ops = [_kv_operands(jnp.concatenate(c, axis=1), precise)
                 for c in ([v_h, zeros_kv, sum_cols[0]], [zeros_kv, v_h, sum_cols[1]])]
        pairs = []
        for pair in range(GROUP // 2):
            slab = (kh * GROUP // 2 + pair) * 2 * HEAD_DIM
            q_pair = q_ref[:, slab:slab + 2 * HEAD_DIM]
            out = None
            inv_denoms = []
            for c in range(2):
                s = jnp.where(mask, _qk(q_pair, *k_ops[c]), NEG)
                sink = sinks_ref[kh * GROUP + 2 * pair + c]
                m = jnp.maximum(jnp.max(s, axis=-1, keepdims=True), sink)
                o = _pv(jnp.exp(s - m), *v_ops[c])
                out = o if out is None else out + o
                inv_denoms.append(jnp.exp(sink - m))
            a2 = out[:, :2 * HEAD_DIM]
            inv_denoms = [1.0 / (out[:, 2 * HEAD_DIM + c:2 * HEAD_DIM + c + 1] + inv_denoms[c]) for c in range(2)]
            pairs.append(a2 * jnp.where(lane < HEAD_DIM, inv_denoms[0], inv_denoms[1]))
        a = jnp.concatenate(pairs, axis=1)
        mix = _sigmoid(ga_ref[:, ocs]) * a + _sigmoid(gp_ref[:, ocs]) * pg
        mix_ref[:, ocs] = mix.astype(mix_ref.dtype)
        if not precise:
            o_ref[:, ocs] = x_ref[:, ocs] + _dot(prev_sc[...], wout_ref[:, ocs])


def _mixer_prompt(q, k, v, u, ga, gp, x, sinks, wpool, pscale, wout, *, layer, batch, seq):
    nb = seq // BLOCK
    m = batch * seq

    def cur(b, s):
        return (b * nb + jnp.minimum(s, nb - 1), 0)

    def prev(b, s):
        return (jnp.maximum(b * nb + jnp.minimum(s, nb - 1) - 1, 0), 0)

    def proj(b, s):
        return (b * nb + jnp.maximum(s - 1, 0), 0)

    return pl.pallas_call(
        functools.partial(_mixer_prompt_kernel, first_block=0, n_blocks=nb),
        grid=(batch, nb + 1),
        in_specs=[
            pl.BlockSpec(memory_space=pltpu.SMEM),
            pl.BlockSpec((BLOCK, D_MODEL), cur),
            pl.BlockSpec((BLOCK, KV_DIM), cur),
            pl.BlockSpec((BLOCK, KV_DIM), prev),
            pl.BlockSpec((BLOCK, KV_DIM), cur),
            pl.BlockSpec((BLOCK, KV_DIM), prev),
            pl.BlockSpec((BLOCK, POOL_DIM), cur),
            pl.BlockSpec((BLOCK, POOL_DIM), prev),
            pl.BlockSpec((BLOCK, D_MODEL), cur),
            pl.BlockSpec((BLOCK, D_MODEL), cur),
            pl.BlockSpec((None, N_KV_HEADS, POOL_GROUP_DIM, POOL_OUT_GROUP), lambda b, n: (layer, 0, 0, 0)),
            pl.BlockSpec((1, D_MODEL), lambda b, n: (0, 0)),
            pl.BlockSpec((BLOCK, D_MODEL), proj),
            pl.BlockSpec((None, D_MODEL, D_MODEL), lambda b, n: (layer, 0, 0)),
        ],
        out_specs=pl.BlockSpec((BLOCK, D_MODEL), proj),
        out_shape=jax.ShapeDtypeStruct((m, D_MODEL), F32),
        scratch_shapes=[pltpu.VMEM((16 + BLOCK, POOL_DIM), F32), pltpu.VMEM((BLOCK, D_MODEL), BF16),
                        pltpu.VMEM((BLOCK, D_MODEL), BF16)],
        compiler_params=pltpu.CompilerParams(
            dimension_semantics=("arbitrary", "arbitrary"), vmem_limit_bytes=VMEM_LIMIT),
        name="mixer_prompt",
    )(sinks, q, k, k, v, v, u, u, ga, gp, wpool, pscale, x, wout)


def _mixer_prompt_tail(q_t, k, v, u, ga_t, gp_t, sinks, wpool, pscale, *, layer, batch, seq):
    nb = seq // BLOCK

    def tail(b, n):
        return (b, 0)

    def cur(b, n):
        return (b * nb + nb - 1, 0)

    def prev(b, n):
        return (jnp.maximum(b * nb + nb - 2, 0), 0)

    return pl.pallas_call(
        functools.partial(_mixer_prompt_kernel, first_block=nb - 1, n_blocks=nb),
        grid=(batch, 1),
        in_specs=[
            pl.BlockSpec(memory_space=pltpu.SMEM),
            pl.BlockSpec((BLOCK, D_MODEL), tail),
            pl.BlockSpec((BLOCK, KV_DIM), cur),
            pl.BlockSpec((BLOCK, KV_DIM), prev),
            pl.BlockSpec((BLOCK, KV_DIM), cur),
            pl.BlockSpec((BLOCK, KV_DIM), prev),
            pl.BlockSpec((BLOCK, POOL_DIM), cur),
            pl.BlockSpec((BLOCK, POOL_DIM), prev),
            pl.BlockSpec((BLOCK, D_MODEL), tail),
            pl.BlockSpec((BLOCK, D_MODEL), tail),
            pl.BlockSpec((None, N_KV_HEADS, POOL_GROUP_DIM, POOL_OUT_GROUP), lambda b, n: (layer, 0, 0, 0)),
            pl.BlockSpec((1, D_MODEL), lambda b, n: (0, 0)),
        ],
        out_specs=pl.BlockSpec((BLOCK, D_MODEL), tail),
        out_shape=jax.ShapeDtypeStruct((batch * BLOCK, D_MODEL), F32),
        scratch_shapes=[pltpu.VMEM((16 + BLOCK, POOL_DIM), F32)],
        compiler_params=pltpu.CompilerParams(
            dimension_semantics=("parallel", "arbitrary"), vmem_limit_bytes=VMEM_LIMIT),
        name="mixer_prompt_tail",
    )(sinks, q_t, k, k, v, v, u, u, ga_t, gp_t, wpool, pscale)


SAMPLE_DB = 8


def _mixer_sample_kernel(sinks_ref, q_ref, kn_ref, vn_ref, u_ref, ga_ref, gp_ref, ck_ref, cv_ref, pool_ref,
                         wpool_ref, pscale_ref, mix_ref, nk_ref, nv_ref, npool_ref, ext_sc, pm_sc, a_sc,
                         *, dec_seq):
    t_new = dec_seq
    rows = SAMPLE_DB * t_new
    precise = q_ref.dtype == F32
    kc_hi, kc_lo = _kv_operands(ck_ref[...].reshape(SAMPLE_DB * WINDOW, KV_DIM), precise)
    vc_hi, vc_lo = _kv_operands(cv_ref[...].reshape(SAMPLE_DB * WINDOW, KV_DIM), precise)
    kn_hi, kn_lo = _kv_operands(kn_ref[...], precise)
    vn_hi, vn_lo = _kv_operands(vn_ref[...], precise)

    rho = lax.broadcasted_iota(jnp.int32, (GROUP * rows, 1), 0)
    q_db = (rho % rows) // t_new
    q_t = rho % t_new
    q_g = rho // rows
    colc = lax.broadcasted_iota(jnp.int32, (1, SAMPLE_DB * WINDOW), 1)
    mask_c = (q_db == colc // WINDOW) & ((colc % WINDOW) >= q_t)
    coln = lax.broadcasted_iota(jnp.int32, (1, rows), 1)
    mask_n = (q_db == coln // t_new) & ((coln % t_new) <= q_t)

    for d in range(SAMPLE_DB):
        rs = slice(d * t_new, (d + 1) * t_new)
        ext_sc[d, 0:POOL_HIST, :] = pool_ref[d]
        ext_sc[d, POOL_HIST:POOL_HIST + t_new, :] = u_ref[rs, :]
        npool_ref[d] = ext_sc[d, t_new:t_new + POOL_HIST, :]
        nk_ref[d, 0:WINDOW - t_new, :] = ck_ref[d, t_new:WINDOW, :]
        nk_ref[d, WINDOW - t_new:WINDOW, :] = kn_ref[rs, :]
        nv_ref[d, 0:WINDOW - t_new, :] = cv_ref[d, t_new:WINDOW, :]
        nv_ref[d, WINDOW - t_new:WINDOW, :] = vn_ref[rs, :]
        for gi, w in enumerate(POOL_WINDOWS):
            pcs = slice(gi * POOL_GROUP_DIM, (gi + 1) * POOL_GROUP_DIM)
            cur = ext_sc[d, POOL_HIST:POOL_HIST + t_new, pcs]
            acc = cur
            for j in range(1, w):
                acc = acc + ext_sc[d, POOL_HIST - j:POOL_HIST - j + t_new, pcs]
            pm_sc[rs, pcs] = acc / float(w) - cur

    for kh in range(N_KV_HEADS):
        hs = slice(kh * HEAD_DIM, (kh + 1) * HEAD_DIM)
        qs = jnp.concatenate(
            [q_ref[:, (kh * GROUP + g) * HEAD_DIM:(kh * GROUP + g + 1) * HEAD_DIM] for g in range(GROUP)], axis=0)
        sc = jnp.where(mask_c, _qk(qs, kc_hi[:, hs], _cols(kc_lo, hs)), NEG)
        sn = jnp.where(mask_n, _qk(qs, kn_hi[:, hs], _cols(kn_lo, hs)), NEG)
        sink = jnp.zeros((GROUP * rows, 1), F32)
        for g in range(GROUP):
            sink = jnp.where(q_g == g, sinks_ref[kh * GROUP + g], sink)
        m = jnp.maximum(jnp.maximum(jnp.max(sc, axis=-1, keepdims=True), jnp.max(sn, axis=-1, keepdims=True)), sink)
        pc = jnp.exp(sc - m)
        pn = jnp.exp(sn - m)
        denom = jnp.sum(pc, axis=-1, keepdims=True) + jnp.sum(pn, axis=-1, keepdims=True) + jnp.exp(sink - m)
        o = (_pv(pc, vc_hi[:, hs], _cols(vc_lo, hs)) + _pv(pn, vn_hi[:, hs], _cols(vn_lo, hs))) / denom
        for g in range(GROUP):
            h = kh * GROUP + g
            a_sc[:, h * HEAD_DIM:(h + 1) * HEAD_DIM] = o[g * rows:(g + 1) * rows, :]

    for gi in range(N_KV_HEADS):
        pcs = slice(gi * POOL_GROUP_DIM, (gi + 1) * POOL_GROUP_DIM)
        ocs = slice(gi * POOL_OUT_GROUP, (gi + 1) * POOL_OUT_GROUP)
        pg = _pool_map(pm_sc[:, pcs], wpool_ref, gi) * pscale_ref[:, ocs]
        mix_ref[:, ocs] = _sigmoid(ga_ref[:, ocs]) * a_sc[:, ocs] + _sigmoid(gp_ref[:, ocs]) * pg


def _mixer_sample(q, kn, vn, u, ga, gp, ck, cv, pool, sinks, wpool, pscale, *, layer, dec_seq):
    db = ck.shape[1]
    rows = SAMPLE_DB * dec_seq
    m = db * dec_seq
    row_blk = lambda i: (i, 0)
    db_blk = lambda i: (i, 0, 0)
    db_layer_blk = lambda i: (layer, i, 0, 0)
    return pl.pallas_call(
        functools.partial(_mixer_sample_kernel, dec_seq=dec_seq),
        grid=(db // SAMPLE_DB,),
        in_specs=[
            pl.BlockSpec(memory_space=pltpu.SMEM),
            pl.BlockSpec((rows, D_MODEL), row_blk),
            pl.BlockSpec((rows, KV_DIM), row_blk),
            pl.BlockSpec((rows, KV_DIM), row_blk),
            pl.BlockSpec((rows, POOL_DIM), row_blk),
            pl.BlockSpec((rows, D_MODEL), row_blk),
            pl.BlockSpec((rows, D_MODEL), row_blk),
            pl.BlockSpec((None, SAMPLE_DB, WINDOW, KV_DIM), db_layer_blk),
            pl.BlockSpec((None, SAMPLE_DB, WINDOW, KV_DIM), db_layer_blk),
            pl.BlockSpec((None, SAMPLE_DB, POOL_HIST, POOL_DIM), db_layer_blk),
            pl.BlockSpec((None, N_KV_HEADS, POOL_GROUP_DIM, POOL_OUT_GROUP), lambda i: (layer, 0, 0, 0)),
            pl.BlockSpec((1, D_MODEL), lambda i: (0, 0)),
        ],
        out_specs=[
            pl.BlockSpec((rows, D_MODEL), row_blk),
            pl.BlockSpec((SAMPLE_DB, WINDOW, KV_DIM), db_blk),
            pl.BlockSpec((SAMPLE_DB, WINDOW, KV_DIM), db_blk),
            pl.BlockSpec((SAMPLE_DB, POOL_HIST, POOL_DIM), db_blk),
        ],
        out_shape=[
            jax.ShapeDtypeStruct((m, D_MODEL), F32),
            jax.ShapeDtypeStruct((db, WINDOW, KV_DIM), F32),
            jax.ShapeDtypeStruct((db, WINDOW, KV_DIM), F32),
            jax.ShapeDtypeStruct((db, POOL_HIST, POOL_DIM), F32),
        ],
        scratch_shapes=[
            pltpu.VMEM((SAMPLE_DB, 24, POOL_DIM), F32),
            pltpu.VMEM((rows, POOL_DIM), F32),
            pltpu.VMEM((rows, D_MODEL), F32),
        ],
        compiler_params=pltpu.CompilerParams(
            dimension_semantics=("arbitrary",), vmem_limit_bytes=VMEM_LIMIT),
        name="mixer_sample_split" if q.dtype == F32 else "mixer_sample",
    )(sinks, q, kn, vn, u, ga, gp, ck, cv, pool, wpool, pscale)


PROJ_TN = 512


def _proj_residual_kernel(a_ref, x_ref, w_ref, o_ref):
    a = a_ref[...]
    a_hi, a_lo = _split(a) if w_ref.dtype == F32 else (a.astype(BF16), None)
    o_ref[...] = x_ref[...] + _mm(a_hi, a_lo, w_ref[...])


def _proj_residual(a, x, w, *, layer):
    m = x.shape[0]
    return pl.pallas_call(
        _proj_residual_kernel,
        grid=(D_MODEL // PROJ_TN,),
        in_specs=[
            pl.BlockSpec((m, D_MODEL), lambda j: (0, 0)),
            pl.BlockSpec((m, PROJ_TN), lambda j: (0, j)),
            pl.BlockSpec((None, D_MODEL, PROJ_TN), lambda j: (layer, 0, j)),
        ],
        out_specs=pl.BlockSpec((m, PROJ_TN), lambda j: (0, j)),
        out_shape=jax.ShapeDtypeStruct((m, D_MODEL), F32),
        compiler_params=pltpu.CompilerParams(
            dimension_semantics=("parallel",), vmem_limit_bytes=VMEM_LIMIT),
        name="proj_residual_split" if w.dtype == F32 else "proj_residual",
    )(a, x, w)


FFN_TF = 512
FFN_TM = 1024


def _ffn_kernel(x_ref, g_ref, wr_ref, br_ref, wg_ref, wu_ref, wd_ref, o_ref, xhi_sc, xlo_sc, comb_sc, *, moe):
    e = pl.program_id(1)
    f = pl.program_id(2)
    precise = wg_ref.dtype == F32

    @pl.when((e == 0) & (f == 0))
    def _():
        x = x_ref[...]
        xn = _rms_norm(x, g_ref[...])
        o_ref[...] = x
        if precise or moe:
            xn_hi, xn_lo = _split(xn)
            xhi_sc[...] = xn_hi
            if precise:
                xlo_sc[...] = xn_lo
        else:
            xhi_sc[...] = xn.astype(BF16)
        if moe:
            logits = _mm(xn_hi, xn_lo, wr_ref[...]) + br_ref[...]
            lane = lax.broadcasted_iota(jnp.int32, logits.shape, 1)
            v1 = jnp.max(logits, axis=-1, keepdims=True)
            i1 = jnp.min(jnp.where(logits == v1, lane, LANES), axis=-1, keepdims=True)
            rest = jnp.where(lane == i1, NEG, logits)
            v2 = jnp.max(rest, axis=-1, keepdims=True)
            i2 = jnp.min(jnp.where(rest == v2, lane, LANES), axis=-1, keepdims=True)
            e2 = jnp.exp(v2 - v1)
            g1 = 1.0 / (1.0 + e2)
            g2 = e2 / (1.0 + e2)
            comb_sc[...] = jnp.where(lane == i1, g1, 0.0) + jnp.where(lane == i2, g2, 0.0)

    xn_hi = xhi_sc[...]
    xn_lo = xlo_sc[...] if precise else None
    gate = _mm(xn_hi, xn_lo, wg_ref[0])
    up = _mm(xn_hi, xn_lo, wu_ref[0])
    h = gate * _sigmoid(gate) * up
    if moe:
        lane = lax.broadcasted_iota(jnp.int32, comb_sc.shape, 1)
        h = h * jnp.sum(jnp.where(lane == e, comb_sc[...], 0.0), axis=-1, keepdims=True)
    h_hi, h_lo = _split(h) if precise else (h.astype(BF16), None)
    o_ref[...] += _mm(h_hi, h_lo, wd_ref[0])


def _ffn(x, g, wr, br, wg, wu, wd, *, layer, moe, tm):
    m = x.shape[0]
    _, n_e, _, d_f = wg.shape
    n_f = d_f // FFN_TF
    lo_rows = tm if wg.dtype == F32 else 16
    return pl.pallas_call(
        functools.partial(_ffn_kernel, moe=moe),
        grid=(m // tm, n_e, n_f),
        in_specs=[
            pl.BlockSpec((tm, D_MODEL), lambda i, e, f: (i, 0), pipeline_mode=pl.Buffered(1)),
            pl.BlockSpec((1, D_MODEL), lambda i, e, f: (0, 0)),
            pl.BlockSpec((None, D_MODEL, LANES), lambda i, e, f: (layer, 0, 0)),
            pl.BlockSpec((None, 1, LANES), lambda i, e, f: (layer, 0, 0)),
            pl.BlockSpec((None, 1, D_MODEL, FFN_TF), lambda i, e, f: (layer, e, 0, f)),
            pl.BlockSpec((None, 1, D_MODEL, FFN_TF), lambda i, e, f: (layer, e, 0, f)),
            pl.BlockSpec((None, 1, FFN_TF, D_MODEL), lambda i, e, f: (layer, e, f, 0)),
        ],
        out_specs=pl.BlockSpec((tm, D_MODEL), lambda i, e, f: (i, 0)),
        out_shape=jax.ShapeDtypeStruct((m, D_MODEL), F32),
        scratch_shapes=[pltpu.VMEM((tm, D_MODEL), BF16), pltpu.VMEM((lo_rows, D_MODEL), BF16),
                        pltpu.VMEM((tm, LANES), F32)],
        compiler_params=pltpu.CompilerParams(
            dimension_semantics=("parallel", "arbitrary", "arbitrary"), vmem_limit_bytes=VMEM_LIMIT),
        name=("ffn_moe" if moe else "ffn_dense") + ("_split" if wg.dtype == F32 else ""),
    )(x, g, wr, br, wg, wu, wd)


MOE_TM = 512
MOE_TF = 1024
MOE_ROUTE_TM = 512
MOE_MOVE_TM = 256


def _route_kernel(x_ref, g_ref, wr_ref, br_ref, o_ref):
    xn_hi, xn_lo = _split(_rms_norm(x_ref[...], g_ref[...]))
    logits = _mm(xn_hi, xn_lo, wr_ref[...]) + br_ref[...]
    lane = lax.broadcasted_iota(jnp.int32, logits.shape, 1)
    v1 = jnp.max(logits, axis=-1, keepdims=True)
    i1 = jnp.min(jnp.where(logits == v1, lane, LANES), axis=-1, keepdims=True)
    rest = jnp.where(lane == i1, NEG, logits)
    v2 = jnp.max(rest, axis=-1, keepdims=True)
    i2 = jnp.min(jnp.where(rest == v2, lane, LANES), axis=-1, keepdims=True)
    e2 = jnp.exp(v2 - v1)
    g1 = 1.0 / (1.0 + e2)
    g2 = e2 / (1.0 + e2)
    o_ref[...] = jnp.where(lane == 0, i1.astype(F32),
                           jnp.where(lane == 1, i2.astype(F32),
                                     jnp.where(lane == 2, g1, jnp.where(lane == 3, g2, 0.0))))


def _route(x, g, wr, br, *, layer):
    m = x.shape[0]
    tm = _row_tile(m, MOE_ROUTE_TM)
    return pl.pallas_call(
        _route_kernel,
        grid=(m // tm,),
        in_specs=[
            pl.BlockSpec((tm, D_MODEL), lambda i: (i, 0)),
            pl.BlockSpec((1, D_MODEL), lambda i: (0, 0)),
            pl.BlockSpec((None, D_MODEL, LANES), lambda i: (layer, 0, 0)),
            pl.BlockSpec((None, 1, LANES), lambda i: (layer, 0, 0)),
        ],
        out_specs=pl.BlockSpec((tm, LANES), lambda i: (i, 0)),
        out_shape=jax.ShapeDtypeStruct((m, LANES), F32),
        compiler_params=pltpu.CompilerParams(dimension_semantics=("parallel",), vmem_limit_bytes=VMEM_LIMIT),
        name="moe_route",
    )(x, g, wr, br)


def _moe_plan(route, n_tiles):
    experts = jnp.arange(N_EXPERTS, dtype=jnp.int32)
    e_pair = route[:, :2].astype(jnp.int32).reshape(-1)
    onehot = (e_pair[:, None] == experts[None, :]).astype(jnp.int32)
    csum = jnp.cumsum(onehot, axis=0)
    rank = jnp.sum(csum * onehot, axis=1) - 1
    tiles_e = (csum[-1] + MOE_TM - 1) // MOE_TM
    tend = jnp.cumsum(tiles_e)
    tstart = tend - tiles_e
    pos = (jnp.sum(onehot * tstart[None, :], axis=1) * MOE_TM + rank).astype(jnp.int32)
    n_used = tend[-1:].astype(jnp.int32)
    t_idx = jnp.arange(n_tiles, dtype=jnp.int32)
    tile_e = jnp.sum((t_idx[:, None] >= tend[None, :]).astype(jnp.int32), axis=1)
    last_e = jnp.max(jnp.where(tiles_e > 0, experts, 0))
    tile_e = jnp.minimum(tile_e, last_e).astype(jnp.int32)
    return pos, tile_e, n_used


def _row_copy(src, dst, sem):
    return pltpu.make_async_copy(src, dst, sem)


def _push_kernel(pos_ref, x_ref, g_ref, zeros_ref, xs_ref, xn_sc, sem):
    del zeros_ref
    tm = x_ref.shape[0]
    base = pl.program_id(0) * tm
    xn_sc[...] = _rms_norm(x_ref[...], g_ref[...])

    def start(r, carry):
        for c in range(2):
            dst = pos_ref[2 * (base + r) + c]
            _row_copy(xn_sc.at[pl.ds(r, 1)], xs_ref.at[pl.ds(dst, 1)], sem).start(priority=c)
        return carry

    lax.fori_loop(0, tm, start, 0, unroll=8)

    def wait(r, carry):
        for c in range(2):
            _row_copy(xn_sc.at[pl.ds(r, 1)], xs_ref.at[pl.ds(0, 1)], sem).wait()
        return carry

    lax.fori_loop(0, tm, wait, 0, unroll=8)


def _push(x, g, pos, n_rows):
    m = x.shape[0]
    tm = _row_tile(m, MOE_MOVE_TM)
    zeros = jnp.zeros((n_rows, D_MODEL), F32)
    return pl.pallas_call(
        _push_kernel,
        grid_spec=pltpu.PrefetchScalarGridSpec(
            num_scalar_prefetch=1,
            grid=(m // tm,),
            in_specs=[
                pl.BlockSpec((tm, D_MODEL), lambda i, pos: (i, 0)),
                pl.BlockSpec((1, D_MODEL), lambda i, pos: (0, 0)),
                pl.BlockSpec(memory_space=pl.ANY),
            ],
            out_specs=pl.BlockSpec(memory_space=pl.ANY),
            scratch_shapes=[pltpu.VMEM((tm, D_MODEL), F32), pltpu.SemaphoreType.DMA],
        ),
        out_shape=jax.ShapeDtypeStruct((n_rows, D_MODEL), F32),
        input_output_aliases={3: 0},
        compiler_params=pltpu.CompilerParams(dimension_semantics=("arbitrary",), vmem_limit_bytes=VMEM_LIMIT),
        name="moe_push",
    )(pos, x, g, zeros)


def _experts_kernel(tile_e_ref, n_used_ref, x_ref, wg_ref, wu_ref, wd_ref, o_ref, xb_sc):
    t = pl.program_id(0)
    f = pl.program_id(1)
    used = t < n_used_ref[0]

    @pl.when(f == 0)
    def _():
        xb_sc[...] = x_ref[...].astype(BF16)
        o_ref[...] = jnp.zeros_like(o_ref)

    @pl.when(used)
    def _():
        xn = xb_sc[...]
        gate = _dot(xn, wg_ref[0])
        up = _dot(xn, wu_ref[0])
        h = gate * _sigmoid(gate) * up
        o_ref[...] += _dot(h.astype(BF16), wd_ref[0])


def _experts(xs, tile_e, n_used, wg, wu, wd, *, layer):
    n_rows = xs.shape[0]
    n_f = wg.shape[3] // MOE_TF
    return pl.pallas_call(
        _experts_kernel,
        grid_spec=pltpu.PrefetchScalarGridSpec(
            num_scalar_prefetch=2,
            grid=(n_rows // MOE_TM, n_f),
            in_specs=[
                pl.BlockSpec((MOE_TM, D_MODEL), lambda t, f, te, nu: (t, 0)),
                pl.BlockSpec((None, 1, D_MODEL, MOE_TF), lambda t, f, te, nu: (layer, te[t], 0, f)),
                pl.BlockSpec((None, 1, D_MODEL, MOE_TF), lambda t, f, te, nu: (layer, te[t], 0, f)),
                pl.BlockSpec((None, 1, MOE_TF, D_MODEL), lambda t, f, te, nu: (layer, te[t], f, 0)),
            ],
            out_specs=pl.BlockSpec((MOE_TM, D_MODEL), lambda t, f, te, nu: (t, 0)),
            scratch_shapes=[pltpu.VMEM((MOE_TM, D_MODEL), BF16)],
        ),
        out_shape=jax.ShapeDtypeStruct((n_rows, D_MODEL), F32),
        compiler_params=pltpu.CompilerParams(
            dimension_semantics=("arbitrary", "arbitrary"), vmem_limit_bytes=VMEM_LIMIT),
        name="moe_experts",
    )(tile_e, n_used, xs, wg, wu, wd)


def _combine_kernel(pos_ref, x_ref, route_ref, ys_ref, o_ref, y_sc, sem):
    tm = x_ref.shape[0]
    base = pl.program_id(0) * tm

    def start(r, carry):
        for c in range(2):
            src = pos_ref[2 * (base + r) + c]
            _row_copy(ys_ref.at[pl.ds(src, 1)], y_sc.at[c, pl.ds(r, 1)], sem).start(priority=c)
        return carry

    lax.fori_loop(0, tm, start, 0, unroll=8)

    def wait(r, carry):
        for c in range(2):
            _row_copy(ys_ref.at[pl.ds(0, 1)], y_sc.at[c, pl.ds(r, 1)], sem).wait()
        return carry

    lax.fori_loop(0, tm, wait, 0, unroll=8)
    o_ref[...] = x_ref[...] + route_ref[:, 2:3] * y_sc[0] + route_ref[:, 3:4] * y_sc[1]


def _combine(x, route, pos, ys):
    m = x.shape[0]
    tm = _row_tile(m, MOE_MOVE_TM)
    return pl.pallas_call(
        _combine_kernel,
        grid_spec=pltpu.PrefetchScalarGridSpec(
            num_scalar_prefetch=1,
            grid=(m // tm,),
            in_specs=[
                pl.BlockSpec((tm, D_MODEL), lambda i, pos: (i, 0)),
                pl.BlockSpec((tm, LANES), lambda i, pos: (i, 0)),
                pl.BlockSpec(memory_space=pl.ANY),
            ],
            out_specs=pl.BlockSpec((tm, D_MODEL), lambda i, pos: (i, 0)),
            scratch_shapes=[pltpu.VMEM((2, tm, D_MODEL), F32), pltpu.SemaphoreType.DMA],
        ),
        out_shape=jax.ShapeDtypeStruct((m, D_MODEL), F32),
        compiler_params=pltpu.CompilerParams(dimension_semantics=("arbitrary",), vmem_limit_bytes=VMEM_LIMIT),
        name="moe_combine",
    )(pos, x, route, ys)


def _moe_routed(x, g, wr, br, wg, wu, wd, *, layer):
    m = x.shape[0]
    n_tiles = (2 * m) // MOE_TM + N_EXPERTS
    route = _route(x, g, wr, br, layer=layer)
    pos, tile_e, n_used = _moe_plan(route, n_tiles)
    xs = _push(x, g, pos, n_tiles * MOE_TM)
    ys = _experts(xs, tile_e, n_used, wg, wu, wd, layer=layer)
    return _combine(x, route, pos, ys)


def _final_norm_kernel(x_ref, g_ref, o_ref):
    o_ref[...] = _rms_norm(x_ref[...], g_ref[...])


def _final_norm(x, g, *, tm):
    m = x.shape[0]
    return pl.pallas_call(
        _final_norm_kernel,
        grid=(m // tm,),
        in_specs=[pl.BlockSpec((tm, D_MODEL), lambda i: (i, 0)), pl.BlockSpec((1, D_MODEL), lambda i: (0, 0))],
        out_specs=pl.BlockSpec((tm, D_MODEL), lambda i: (i, 0)),
        out_shape=jax.ShapeDtypeStruct((m, D_MODEL), F32),
        compiler_params=pltpu.CompilerParams(dimension_semantics=("parallel",), vmem_limit_bytes=VMEM_LIMIT),
        name="final_norm",
    )(x, g)


def _rope_tables(pos):
    half = HEAD_DIM // 2
    inv = ROPE_THETA ** (-2.0 * jnp.arange(half, dtype=F32) / HEAD_DIM)
    ang = pos.astype(F32)[:, None] * inv[None, :]
    cos, sin = jnp.cos(ang), jnp.sin(ang)
    reps = LANES // HEAD_DIM
    return (jnp.tile(jnp.concatenate([cos, cos], axis=-1), (1, reps)),
            jnp.tile(jnp.concatenate([-sin, sin], axis=-1), (1, reps)))


def _row_tile(m, cap):
    t = cap
    while m % t:
        t //= 2
    return t


def _put_tail(main, tail, batch, seq):
    c = main.shape[1]
    upd = tail.reshape(batch, BLOCK, c).astype(main.dtype)
    return lax.dynamic_update_slice(main.reshape(batch, seq, c), upd, (0, seq - BLOCK, 0)).reshape(batch * seq, c)


def kernel(x_prompt, x_sample, cache_k, cache_v, state_pool, norm_mix, norm_ffn, norm_final, w_in, attn_sinks,
           w_pool_map, pool_scale, w_out, w_ffn_gate, w_ffn_up, w_ffn_down, w_router, b_router, w_exp_gate,
           w_exp_up, w_exp_down):
    batch, seq, _ = x_prompt.shape
    dec_batch, dec_seq, _ = x_sample.shape
    depth = w_in.shape[0]
    assert seq % BLOCK == 0 and dec_batch % SAMPLE_DB == 0 and dec_seq <= POOL_HIST

    w_in_b = w_in.astype(BF16)
    w_pool_b = w_pool_map.astype(BF16)
    w_out_b = w_out.astype(BF16)
    w_fg, w_fu, w_fd = (w[:, None] for w in (w_ffn_gate, w_ffn_up, w_ffn_down))
    w_fg_b, w_fu_b, w_fd_b = (w.astype(BF16) for w in (w_fg, w_fu, w_fd))
    w_eg_b, w_eu_b, w_ed_b = (w.astype(BF16) for w in (w_exp_gate, w_exp_up, w_exp_down))
    w_r = jnp.pad(w_router.astype(F32), ((0, 0), (0, 0), (0, LANES - N_EXPERTS)))
    b_r = jnp.pad(b_router.astype(F32), ((0, 0), (0, LANES - N_EXPERTS)), constant_values=NEG)[:, None]
    zero_wr = jnp.zeros((w_fg.shape[0], D_MODEL, LANES), F32)
    zero_br = jnp.zeros((w_fg.shape[0], 1, LANES), F32)

    pos_p = jnp.arange(seq, dtype=jnp.int32)
    pos_s = jnp.tile(PAST_LEN + jnp.arange(dec_seq, dtype=jnp.int32), dec_batch)
    cos_p, sin_p = _rope_tables(pos_p)
    cos_s, sin_s = _rope_tables(pos_s)
    cos_x, sin_x = _rope_tables(jnp.concatenate([jnp.tile(pos_p[seq - BLOCK:], batch), pos_s]))

    m_p, m_s, m_t = batch * seq, dec_batch * dec_seq, batch * BLOCK
    tm_in_p = _row_tile(seq, IN_TM)
    tm_ffn_p = _row_tile(m_p, FFN_TM)

    xp = x_prompt.reshape(m_p, D_MODEL)
    xs = x_sample.reshape(m_s, D_MODEL)
    ck = cache_k.reshape(depth, dec_batch, WINDOW, KV_DIM)
    cv = cache_v.reshape(depth, dec_batch, WINDOW, KV_DIM)

    n_split = min(2, depth)
    xx = jnp.concatenate([x_prompt[:, seq - BLOCK:].reshape(m_t, D_MODEL), xs], axis=0)

    kp_l, vp_l, pp_l, ks_l, vs_l, ps_l = [], [], [], [], [], []
    for l in range(depth):
        g_mix = norm_mix[l][None]
        g_ffn = norm_ffn[l][None]
        pscale = pool_scale[l][None]
        sinks = attn_sinks[l]
        split = l < n_split

        q, k, v, u, ga, gp = _inproj(xp, g_mix, w_in_b, cos_p, sin_p, layer=l, tm=tm_in_p)
        if split:
            qx, kx, vx, ux, gax, gpx = _inproj(xx, g_mix, w_in, cos_x, sin_x, layer=l, tm=m_t + m_s)
            k, v, u = (_put_tail(a, b[:m_t], batch, seq) for a, b in ((k, kx), (v, vx), (u, ux)))
        xp = _mixer_prompt(q, k, v, u, ga, gp, xp, sinks, w_pool_b, pscale, w_out_b, layer=l, batch=batch, seq=seq)
        if split:
            mix_t = _mixer_prompt_tail(qx[:m_t], k, v, u, gax[:m_t], gpx[:m_t], sinks, w_pool_map, pscale,
                                       layer=l, batch=batch, seq=seq)
            mix_s, nk, nv, npool = _mixer_sample(qx[m_t:], kx[m_t:], vx[m_t:], ux[m_t:], gax[m_t:], gpx[m_t:],
                                                 ck, cv, state_pool, sinks, w_pool_map, pscale,
                                                 layer=l, dec_seq=dec_seq)
            xx = _proj_residual(jnp.concatenate([mix_t, mix_s], axis=0), xx, w_out, layer=l)
            xp = _put_tail(xp, xx[:m_t], batch, seq)
        else:
            qs, ks, vs, us, gas, gps = _inproj(xs, g_mix, w_in_b, cos_s, sin_s, layer=l, tm=m_s)
            mix_s, nk, nv, npool = _mixer_sample(qs, ks, vs, us, gas, gps, ck, cv, state_pool, sinks,
                                                 w_pool_b, pscale, layer=l, dec_seq=dec_seq)
            xs = _proj_residual(mix_s, xs, w_out_b, layer=l)
        kp_l.append(k.reshape(batch, seq, N_KV_HEADS, HEAD_DIM)[:, seq - WINDOW:])
        vp_l.append(v.reshape(batch, seq, N_KV_HEADS, HEAD_DIM)[:, seq - WINDOW:])
        pp_l.append(u.reshape(batch, seq, POOL_DIM)[:, seq - POOL_HIST:])
        ks_l.append(nk.reshape(dec_batch, WINDOW, N_KV_HEADS, HEAD_DIM))
        vs_l.append(nv.reshape(dec_batch, WINDOW, N_KV_HEADS, HEAD_DIM))
        ps_l.append(npool)

        i = l // 2
        if l % 2 == 0:
            xp = _ffn(xp, g_ffn, zero_wr, zero_br, w_fg_b, w_fu_b, w_fd_b, layer=i, moe=False, tm=tm_ffn_p)
            if split and l + 1 < n_split:
                xx = _ffn(xx, g_ffn, zero_wr, zero_br, w_fg, w_fu, w_fd, layer=i, moe=False, tm=m_t + m_s)
                xp = _put_tail(xp, xx[:m_t], batch, seq)
            else:
                if split:
                    xs = xx[m_t:]
                xs = _ffn(xs, g_ffn, zero_wr, zero_br, w_fg_b, w_fu_b, w_fd_b, layer=i, moe=False, tm=m_s)
        else:
            if split:
                xs = xx[m_t:]
            xp = _moe_routed(xp, g_ffn, w_r, b_r, w_eg_b, w_eu_b, w_ed_b, layer=i)
            xs = _ffn(xs, g_ffn, w_r, b_r, w_eg_b, w_eu_b, w_ed_b, layer=i, moe=True, tm=m_s)

    g_fin = norm_final[None]
    y_p = _final_norm(xp, g_fin, tm=tm_ffn_p).reshape(batch, seq, D_MODEL)
    y_s = _final_norm(xs, g_fin, tm=m_s).reshape(dec_batch, dec_seq, D_MODEL)
    return (y_p, y_s, jnp.stack(kp_l), jnp.stack(vp_l), jnp.stack(pp_l),
            jnp.stack(ks_l), jnp.stack(vs_l), jnp.stack(ps_l))
```

```python
import functools

import jax
import jax.numpy as jnp
from jax import lax
from jax.experimental import pallas as pl
from jax.experimental.pallas import tpu as pltpu

D_MODEL = 2048
HEAD_DIM = 64
N_HEADS = 32
N_KV_HEADS = 4
GROUP = N_HEADS // N_KV_HEADS
KV_DIM = N_KV_HEADS * HEAD_DIM
WINDOW = 128
BLOCK = 128
ROPE_THETA = 10000.0
PAST_LEN = 16384
POOL_WINDOWS = (2, 4, 8, 16)
POOL_GROUP_DIM = 256
POOL_DIM = 1024
POOL_OUT_GROUP = 512
POOL_HIST = 15
N_EXPERTS = 8
EPS = 1e-5
NEG = -1e30

BF16 = jnp.bfloat16
F32 = jnp.float32

LANES = 128
VMEM_LIMIT = 52 * 1024 * 1024

IN_TN = 512
IN_TM = 1024
IN_Q_TILES = D_MODEL // IN_TN
IN_KV_TILE = IN_Q_TILES
IN_U_START = IN_KV_TILE + 1
IN_U_TILES = POOL_DIM // IN_TN
IN_GA_START = IN_U_START + IN_U_TILES
IN_G_TILES = D_MODEL // IN_TN
IN_GP_START = IN_GA_START + IN_G_TILES
IN_TILES = IN_GP_START + IN_G_TILES


def _sigmoid(x):
    return 1.0 / (1.0 + jnp.exp(-x))


def _split(a):
    hi = a.astype(BF16)
    lo = (a - hi.astype(F32)).astype(BF16)
    return hi, lo


def _dot(a, b):
    return jnp.dot(a, b, preferred_element_type=F32)


def _dot_nt(a, b):
    return lax.dot_general(a, b, (((1,), (1,)), ((), ())), preferred_element_type=F32)


def _mm(a_hi, a_lo, w):
    if w.dtype == BF16:
        return _dot(a_hi, w)
    w_hi, w_lo = _split(w)
    return _dot(a_hi, w_hi) + (_dot(a_lo, w_hi) + _dot(a_hi, w_lo))


def _qk(q, k_hi, k_lo):
    if q.dtype == BF16:
        return _dot_nt(q, k_hi)
    q_hi, q_lo = _split(q)
    return _dot_nt(q_hi, k_hi) + (_dot_nt(q_lo, k_hi) + _dot_nt(q_hi, k_lo))


def _pv(p, v_hi, v_lo):
    if v_lo is None:
        return _dot(p.astype(BF16), v_hi)
    p_hi, p_lo = _split(p)
    return _dot(p_hi, v_hi) + (_dot(p_lo, v_hi) + _dot(p_hi, v_lo))


def _kv_operands(t, precise):
    return _split(t) if precise else (t.astype(BF16), None)


def _cols(t, c):
    return None if t is None else t[:, c]


def _pool_map(pm, w_ref, gi):
    pm_hi, pm_lo = _split(pm) if w_ref.dtype == F32 else (pm.astype(BF16), None)
    return _mm(pm_hi, pm_lo, w_ref[gi])


def _rms_norm(x, g):
    ms = jnp.mean(x * x, axis=-1, keepdims=True)
    return x * lax.rsqrt(ms + EPS) * g


def _rope(t, cos, sin_signed):
    width = t.shape[1]
    lane = lax.broadcasted_iota(jnp.int32, t.shape, 1)
    first_half = (lane % HEAD_DIM) < (HEAD_DIM // 2)
    rot = jnp.where(first_half, pltpu.roll(t, width - HEAD_DIM // 2, 1), pltpu.roll(t, HEAD_DIM // 2, 1))
    reps = width // cos.shape[1]
    return t * jnp.tile(cos, (1, reps)) + rot * jnp.tile(sin_signed, (1, reps))


def _inproj_kernel(x_ref, g_ref, w_ref, cos_ref, sin_ref, q_ref, k_ref, v_ref, u_ref, ga_ref, gp_ref,
                   xhi_sc, xlo_sc):
    j = pl.program_id(1)
    precise = w_ref.dtype == F32

    @pl.when(j == 0)
    def _():
        xn = _rms_norm(x_ref[...], g_ref[...])
        if precise:
            xhi_sc[...], xlo_sc[...] = _split(xn)
        else:
            xhi_sc[...] = xn.astype(BF16)

    def z():
        return _mm(xhi_sc[...], xlo_sc[...] if precise else None, w_ref[...])

    @pl.when(j < IN_Q_TILES)
    def _():
        q_ref[...] = (_rope(z(), cos_ref[...], sin_ref[...]) * (HEAD_DIM ** -0.5)).astype(q_ref.dtype)

    @pl.when(j == IN_KV_TILE)
    def _():
        kv = z()
        k_ref[...] = _rope(kv[:, :KV_DIM], cos_ref[...], sin_ref[...])
        v_ref[...] = kv[:, KV_DIM:]

    @pl.when((j >= IN_U_START) & (j < IN_GA_START))
    def _():
        u_ref[...] = z()

    @pl.when((j >= IN_GA_START) & (j < IN_GP_START))
    def _():
        ga_ref[...] = z()

    @pl.when(j >= IN_GP_START)
    def _():
        gp_ref[...] = z()


def _inproj(x, g, w, cos, sin, *, layer, tm):
    m = x.shape[0]
    pos_blocks = cos.shape[0] // tm
    grid = (m // tm, IN_TILES)

    def clip(j, lo, n):
        return jnp.clip(j - lo, 0, n - 1)

    lo_rows = tm if w.dtype == F32 else 16
    return pl.pallas_call(
        _inproj_kernel,
        grid=grid,
        in_specs=[
            pl.BlockSpec((tm, D_MODEL), lambda i, j: (i, 0), pipeline_mode=pl.Buffered(1)),
            pl.BlockSpec((1, D_MODEL), lambda i, j: (0, 0)),
            pl.BlockSpec((None, D_MODEL, IN_TN), lambda i, j: (layer, 0, j)),
            pl.BlockSpec((tm, LANES), lambda i, j: (i % pos_blocks, 0)),
            pl.BlockSpec((tm, LANES), lambda i, j: (i % pos_blocks, 0)),
        ],
        out_specs=[
            pl.BlockSpec((tm, IN_TN), lambda i, j: (i, clip(j, 0, IN_Q_TILES))),
            pl.BlockSpec((tm, KV_DIM), lambda i, j: (i, 0)),
            pl.BlockSpec((tm, KV_DIM), lambda i, j: (i, 0)),
            pl.BlockSpec((tm, IN_TN), lambda i, j: (i, clip(j, IN_U_START, IN_U_TILES))),
            pl.BlockSpec((tm, IN_TN), lambda i, j: (i, clip(j, IN_GA_START, IN_G_TILES))),
            pl.BlockSpec((tm, IN_TN), lambda i, j: (i, clip(j, IN_GP_START, IN_G_TILES))),
        ],
        out_shape=[
            jax.ShapeDtypeStruct((m, D_MODEL), w.dtype),
            jax.ShapeDtypeStruct((m, KV_DIM), F32),
            jax.ShapeDtypeStruct((m, KV_DIM), F32),
            jax.ShapeDtypeStruct((m, POOL_DIM), F32),
            jax.ShapeDtypeStruct((m, D_MODEL), F32),
            jax.ShapeDtypeStruct((m, D_MODEL), F32),
        ],
        scratch_shapes=[pltpu.VMEM((tm, D_MODEL), BF16), pltpu.VMEM((lo_rows, D_MODEL), BF16)],
        compiler_params=pltpu.CompilerParams(
            dimension_semantics=("parallel", "arbitrary"), vmem_limit_bytes=VMEM_LIMIT),
        name="inproj_split" if w.dtype == F32 else "inproj",
    )(x, g, w, cos, sin)


def _mixer_prompt_kernel(sinks_ref, q_ref, kc_ref, kp_ref, vc_ref, vp_ref, uc_ref, up_ref, ga_ref, gp_ref,
                         wpool_ref, pscale_ref, *rest, first_block, n_blocks):
    precise = q_ref.dtype == F32
    if precise:
        mix_ref, ext_sc = rest
        n = first_block + pl.program_id(1)
    else:
        x_ref, *wout_refs, o_ref, ext_sc, mix_ref, prev_sc = rest
        step = pl.program_id(1)
        n = jnp.minimum(step, n_blocks - 1)

        @pl.when((pl.program_id(0) == 0) & (step == 0))
        def _():
            mix_ref[...] = jnp.zeros_like(mix_ref)

        prev_sc[...] = mix_ref[...]
    not_first = n > 0

    kb = jnp.concatenate([kp_ref[...], kc_ref[...]], axis=0)
    vb = jnp.concatenate([vp_ref[...], vc_ref[...]], axis=0)
    qi = lax.broadcasted_iota(jnp.int32, (BLOCK, 2 * BLOCK), 0)
    kj = lax.broadcasted_iota(jnp.int32, (BLOCK, 2 * BLOCK), 1)
    dist = qi + BLOCK - kj
    mask = (dist >= 0) & (dist <= WINDOW) & ((kj >= BLOCK) | not_first)
    lane = lax.broadcasted_iota(jnp.int32, (BLOCK, 2 * HEAD_DIM), 1)
    zeros_kv = jnp.zeros((2 * BLOCK, HEAD_DIM), F32)
    sum_cols = [(lax.broadcasted_iota(jnp.int32, (2 * BLOCK, 2 * HEAD_DIM), 1) == c).astype(F32) for c in range(2)]

    ext_sc[0:16, :] = jnp.where(not_first, up_ref[BLOCK - 16:BLOCK, :], 0.0)
    ext_sc[16:16 + BLOCK, :] = uc_ref[...]
    pos = n * BLOCK + lax.broadcasted_iota(jnp.int32, (BLOCK, 1), 0)

    for kh in range(N_KV_HEADS):
        w = POOL_WINDOWS[kh]
        pcs = slice(kh * POOL_GROUP_DIM, (kh + 1) * POOL_GROUP_DIM)
        cur = ext_sc[16:16 + BLOCK, pcs]
        acc = cur
        for d in range(1, w):
            acc = acc + ext_sc[16 - d:16 - d + BLOCK, pcs]
        inv_cnt = 1.0 / jnp.minimum(pos + 1, w).astype(F32)
        pm = acc * inv_cnt - cur
        ocs = slice(kh * POOL_OUT_GROUP, (kh + 1) * POOL_OUT_GROUP)
        pg = _pool_map(pm, wpool_ref, kh) * pscale_ref[:, ocs]

        hs = slice(kh * HEAD_DIM, (kh + 1) * HEAD_DIM)
        k_h, v_h = kb[:, hs], vb[:, hs]
        k_op = _kv_operands(jnp.concatenate([jnp.concatenate([k_h, zeros_kv], axis=1),
                                             jnp.concatenate([zeros_kv, k_h], axis=1)], axis=0), precise)
        v_op = _kv_operands(jnp.concatenate([jnp.concatenate([v_h, zeros_kv, sum_cols[0]], axis=1),
                                             jnp.concatenate([zeros_kv, v_h, sum_cols[1]], axis=1)], axis=0), precise)
        pairs = []
        for pair in range(GROUP // 2):
            slab = (kh * GROUP // 2 + pair) * 2 * HEAD_DIM
            s2 = _qk(q_ref[:, slab:slab + 2 * HEAD_DIM], *k_op)
            probs, sink_terms = [], []
            for c in range(2):
                s = jnp.where(mask, s2[:, c * 2 * BLOCK:(c + 1) * 2 * BLOCK], NEG)
                sink = sinks_ref[kh * GROUP + 2 * pair + c]
                m = jnp.maximum(jnp.max(s, axis=-1, keepdims=True), sink)
                probs.append(jnp.exp(s - m))
                sink_terms.append(jnp.exp(sink - m))
            out = _pv(jnp.concatenate(probs, axis=1), *v_op)
            inv = [1.0 / (out[:, 2 * HEAD_DIM + c:2 * HEAD_DIM + c + 1] + sink_terms[c]) for c in range(2)]
            pairs.append(out[:, :2 * HEAD_DIM] * jnp.where(lane < HEAD_DIM, inv[0], inv[1]))
        a = jnp.concatenate(pairs, axis=1)
        mix = _sigmoid(ga_ref[:, ocs]) * a + _sigmoid(gp_ref[:, ocs]) * pg
        mix_ref[:, ocs] = mix.astype(mix_ref.dtype)
        if not precise:
            o_ref[:, ocs] = x_ref[:, ocs] + _dot(prev_sc[...], wout_refs[kh][...])


def _mixer_prompt(q, k, v, u, ga, gp, x, sinks, wpool, pscale, wout, *, layer, batch, seq):
    nb = seq // BLOCK
    m = batch * seq

    def cur(b, s):
        return (b * nb + jnp.minimum(s, nb - 1), 0)

    def prev(b, s):
        return (jnp.maximum(b * nb + jnp.minimum(s, nb - 1) - 1, 0), 0)

    def proj(b, s):
        return (b * nb + jnp.maximum(s - 1, 0), 0)

    return pl.pallas_call(
        functools.partial(_mixer_prompt_kernel, first_block=0, n_blocks=nb),
        grid=(batch, nb + 1),
        in_specs=[
            pl.BlockSpec(memory_space=pltpu.SMEM),
            pl.BlockSpec((BLOCK, D_MODEL), cur),
            pl.BlockSpec((BLOCK, KV_DIM), cur),
            pl.BlockSpec((BLOCK, KV_DIM), prev),
            pl.BlockSpec((BLOCK, KV_DIM), cur),
            pl.BlockSpec((BLOCK, KV_DIM), prev),
            pl.BlockSpec((BLOCK, POOL_DIM), cur),
            pl.BlockSpec((BLOCK, POOL_DIM), prev),
            pl.BlockSpec((BLOCK, D_MODEL), cur),
            pl.BlockSpec((BLOCK, D_MODEL), cur),
            pl.BlockSpec((None, N_KV_HEADS, POOL_GROUP_DIM, POOL_OUT_GROUP), lambda b, n: (layer, 0, 0, 0)),
            pl.BlockSpec((1, D_MODEL), lambda b, n: (0, 0)),
            pl.BlockSpec((BLOCK, D_MODEL), proj),
            *[pl.BlockSpec((None, D_MODEL, POOL_OUT_GROUP), functools.partial(lambda b, n, c: (layer, 0, c), c=c))
              for c in range(N_KV_HEADS)],
        ],
        out_specs=pl.BlockSpec((BLOCK, D_MODEL), proj),
        out_shape=jax.ShapeDtypeStruct((m, D_MODEL), F32),
        scratch_shapes=[pltpu.VMEM((16 + BLOCK, POOL_DIM), F32), pltpu.VMEM((BLOCK, D_MODEL), BF16),
                        pltpu.VMEM((BLOCK, D_MODEL), BF16)],
        compiler_params=pltpu.CompilerParams(
            dimension_semantics=("arbitrary", "arbitrary"), vmem_limit_bytes=VMEM_LIMIT),
        name="mixer_prompt",
    )(sinks, q, k, k, v, v, u, u, ga, gp, wpool, pscale, x, *([wout] * N_KV_HEADS))


def _mixer_prompt_tail(q_t, k, v, u, ga_t, gp_t, sinks, wpool, pscale, *, layer, batch, seq):
    nb = seq // BLOCK

    def tail(b, n):
        return (b, 0)

    def cur(b, n):
        return (b * nb + nb - 1, 0)

    def prev(b, n):
        return (jnp.maximum(b * nb + nb - 2, 0), 0)

    return pl.pallas_call(
        functools.partial(_mixer_prompt_kernel, first_block=nb - 1, n_blocks=nb),
        grid=(batch, 1),
        in_specs=[
            pl.BlockSpec(memory_space=pltpu.SMEM),
            pl.BlockSpec((BLOCK, D_MODEL), tail),
            pl.BlockSpec((BLOCK, KV_DIM), cur),
            pl.BlockSpec((BLOCK, KV_DIM), prev),
            pl.BlockSpec((BLOCK, KV_DIM), cur),
            pl.BlockSpec((BLOCK, KV_DIM), prev),
            pl.BlockSpec((BLOCK, POOL_DIM), cur),
            pl.BlockSpec((BLOCK, POOL_DIM), prev),
            pl.BlockSpec((BLOCK, D_MODEL), tail),
            pl.BlockSpec((BLOCK, D_MODEL), tail),
            pl.BlockSpec((None, N_KV_HEADS, POOL_GROUP_DIM, POOL_OUT_GROUP), lambda b, n: (layer, 0, 0, 0)),
            pl.BlockSpec((1, D_MODEL), lambda b, n: (0, 0)),
        ],
        out_specs=pl.BlockSpec((BLOCK, D_MODEL), tail),
        out_shape=jax.ShapeDtypeStruct((batch * BLOCK, D_MODEL), F32),
        scratch_shapes=[pltpu.VMEM((16 + BLOCK, POOL_DIM), F32)],
        compiler_params=pltpu.CompilerParams(
            dimension_semantics=("parallel", "arbitrary"), vmem_limit_bytes=VMEM_LIMIT),
        name="mixer_prompt_tail",
    )(sinks, q_t, k, k, v, v, u, u, ga_t, gp_t, wpool, pscale)


SAMPLE_DB = 8


def _mixer_sample_kernel(sinks_ref, q_ref, kn_ref, vn_ref, u_ref, ga_ref, gp_ref, ck_ref, cv_ref, pool_ref,
                         wpool_ref, pscale_ref, mix_ref, nk_ref, nv_ref, npool_ref, ext_sc, pm_sc, a_sc,
                         *, dec_seq):
    t_new = dec_seq
    rows = SAMPLE_DB * t_new
    precise = q_ref.dtype == F32
    kc_hi, kc_lo = _kv_operands(ck_ref[...].reshape(SAMPLE_DB * WINDOW, KV_DIM), precise)
    vc_hi, vc_lo = _kv_operands(cv_ref[...].reshape(SAMPLE_DB * WINDOW, KV_DIM), precise)
    kn_hi, kn_lo = _kv_operands(kn_ref[...], precise)
    vn_hi, vn_lo = _kv_operands(vn_ref[...], precise)

    rho = lax.broadcasted_iota(jnp.int32, (GROUP * rows, 1), 0)
    q_db = (rho % rows) // t_new
    q_t = rho % t_new
    q_g = rho // rows
    colc = lax.broadcasted_iota(jnp.int32, (1, SAMPLE_DB * WINDOW), 1)
    mask_c = (q_db == colc // WINDOW) & ((colc % WINDOW) >= q_t)
    coln = lax.broadcasted_iota(jnp.int32, (1, rows), 1)
    mask_n = (q_db == coln // t_new) & ((coln % t_new) <= q_t)

    for d in range(SAMPLE_DB):
        rs = slice(d * t_new, (d + 1) * t_new)
        ext_sc[d, 0:POOL_HIST, :] = pool_ref[d]
        ext_sc[d, POOL_HIST:POOL_HIST + t_new, :] = u_ref[rs, :]
        npool_ref[d] = ext_sc[d, t_new:t_new + POOL_HIST, :]
        nk_ref[d, 0:WINDOW - t_new, :] = ck_ref[d, t_new:WINDOW, :]
        nk_ref[d, WINDOW - t_new:WINDOW, :] = kn_ref[rs, :]
        nv_ref[d, 0:WINDOW - t_new, :] = cv_ref[d, t_new:WINDOW, :]
        nv_ref[d, WINDOW - t_new:WINDOW, :] = vn_ref[rs, :]
        for gi, w in enumerate(POOL_WINDOWS):
            pcs = slice(gi * POOL_GROUP_DIM, (gi + 1) * POOL_GROUP_DIM)
            cur = ext_sc[d, POOL_HIST:POOL_HIST + t_new, pcs]
            acc = cur
            for j in range(1, w):
                acc = acc + ext_sc[d, POOL_HIST - j:POOL_HIST - j + t_new, pcs]
            pm_sc[rs, pcs] = acc / float(w) - cur

    for kh in range(N_KV_HEADS):
        hs = slice(kh * HEAD_DIM, (kh + 1) * HEAD_DIM)
        qs = jnp.concatenate(
            [q_ref[:, (kh * GROUP + g) * HEAD_DIM:(kh * GROUP + g + 1) * HEAD_DIM] for g in range(GROUP)], axis=0)
        sc = jnp.where(mask_c, _qk(qs, kc_hi[:, hs], _cols(kc_lo, hs)), NEG)
        sn = jnp.where(mask_n, _qk(qs, kn_hi[:, hs], _cols(kn_lo, hs)), NEG)
        sink = jnp.zeros((GROUP * rows, 1), F32)
        for g in range(GROUP):
            sink = jnp.where(q_g == g, sinks_ref[kh * GROUP + g], sink)
        m = jnp.maximum(jnp.maximum(jnp.max(sc, axis=-1, keepdims=True), jnp.max(sn, axis=-1, keepdims=True)), sink)
        pc = jnp.exp(sc - m)
        pn = jnp.exp(sn - m)
        denom = jnp.sum(pc, axis=-1, keepdims=True) + jnp.sum(pn, axis=-1, keepdims=True) + jnp.exp(sink - m)
        o = (_pv(pc, vc_hi[:, hs], _cols(vc_lo, hs)) + _pv(pn, vn_hi[:, hs], _cols(vn_lo, hs))) / denom
        for g in range(GROUP):
            h = kh * GROUP + g
            a_sc[:, h * HEAD_DIM:(h + 1) * HEAD_DIM] = o[g * rows:(g + 1) * rows, :]

    for gi in range(N_KV_HEADS):
        pcs = slice(gi * POOL_GROUP_DIM, (gi + 1) * POOL_GROUP_DIM)
        ocs = slice(gi * POOL_OUT_GROUP, (gi + 1) * POOL_OUT_GROUP)
        pg = _pool_map(pm_sc[:, pcs], wpool_ref, gi) * pscale_ref[:, ocs]
        mix_ref[:, ocs] = _sigmoid(ga_ref[:, ocs]) * a_sc[:, ocs] + _sigmoid(gp_ref[:, ocs]) * pg


def _mixer_sample(q, kn, vn, u, ga, gp, ck, cv, pool, sinks, wpool, pscale, *, layer, dec_seq):
    db = ck.shape[1]
    rows = SAMPLE_DB * dec_seq
    m = db * dec_seq
    row_blk = lambda i: (i, 0)
    db_blk = lambda i: (i, 0, 0)
    db_layer_blk = lambda i: (layer, i, 0, 0)
    return pl.pallas_call(
        functools.partial(_mixer_sample_kernel, dec_seq=dec_seq),
        grid=(db // SAMPLE_DB,),
        in_specs=[
            pl.BlockSpec(memory_space=pltpu.SMEM),
            pl.BlockSpec((rows, D_MODEL), row_blk),
            pl.BlockSpec((rows, KV_DIM), row_blk),
            pl.BlockSpec((rows, KV_DIM), row_blk),
            pl.BlockSpec((rows, POOL_DIM), row_blk),
            pl.BlockSpec((rows, D_MODEL), row_blk),
            pl.BlockSpec((rows, D_MODEL), row_blk),
            pl.BlockSpec((None, SAMPLE_DB, WINDOW, KV_DIM), db_layer_blk),
            pl.BlockSpec((None, SAMPLE_DB, WINDOW, KV_DIM), db_layer_blk),
            pl.BlockSpec((None, SAMPLE_DB, POOL_HIST, POOL_DIM), db_layer_blk),
            pl.BlockSpec((None, N_KV_HEADS, POOL_GROUP_DIM, POOL_OUT_GROUP), lambda i: (layer, 0, 0, 0)),
            pl.BlockSpec((1, D_MODEL), lambda i: (0, 0)),
        ],
        out_specs=[
            pl.BlockSpec((rows, D_MODEL), row_blk),
            pl.BlockSpec((SAMPLE_DB, WINDOW, KV_DIM), db_blk),
            pl.BlockSpec((SAMPLE_DB, WINDOW, KV_DIM), db_blk),
            pl.BlockSpec((SAMPLE_DB, POOL_HIST, POOL_DIM), db_blk),
        ],
        out_shape=[
            jax.ShapeDtypeStruct((m, D_MODEL), F32),
            jax.ShapeDtypeStruct((db, WINDOW, KV_DIM), F32),
            jax.ShapeDtypeStruct((db, WINDOW, KV_DIM), F32),
            jax.ShapeDtypeStruct((db, POOL_HIST, POOL_DIM), F32),
        ],
        scratch_shapes=[
            pltpu.VMEM((SAMPLE_DB, 24, POOL_DIM), F32),
            pltpu.VMEM((rows, POOL_DIM), F32),
            pltpu.VMEM((rows, D_MODEL), F32),
        ],
        compiler_params=pltpu.CompilerParams(
            dimension_semantics=("arbitrary",), vmem_limit_bytes=VMEM_LIMIT),
        name="mixer_sample_split" if q.dtype == F32 else "mixer_sample",
    )(sinks, q, kn, vn, u, ga, gp, ck, cv, pool, wpool, pscale)


PROJ_TN = 512


def _proj_residual_kernel(a_ref, x_ref, w_ref, o_ref):
    a = a_ref[...]
    a_hi, a_lo = _split(a) if w_ref.dtype == F32 else (a.astype(BF16), None)
    o_ref[...] = x_ref[...] + _mm(a_hi, a_lo, w_ref[...])


def _proj_residual(a, x, w, *, layer):
    m = x.shape[0]
    return pl.pallas_call(
        _proj_residual_kernel,
        grid=(D_MODEL // PROJ_TN,),
        in_specs=[
            pl.BlockSpec((m, D_MODEL), lambda j: (0, 0)),
            pl.BlockSpec((m, PROJ_TN), lambda j: (0, j)),
            pl.BlockSpec((None, D_MODEL, PROJ_TN), lambda j: (layer, 0, j)),
        ],
        out_specs=pl.BlockSpec((m, PROJ_TN), lambda j: (0, j)),
        out_shape=jax.ShapeDtypeStruct((m, D_MODEL), F32),
        compiler_params=pltpu.CompilerParams(
            dimension_semantics=("parallel",), vmem_limit_bytes=VMEM_LIMIT),
        name="proj_residual_split" if w.dtype == F32 else "proj_residual",
    )(a, x, w)


FFN_TF = 512
FFN_TM = 512
FFN_TF_SAMPLE = 1024


def _ffn_kernel(x_ref, g_ref, wr_ref, br_ref, wg_ref, wu_ref, wd_ref, o_ref, xhi_sc, xlo_sc, comb_sc, *, moe):
    e = pl.program_id(1)
    f = pl.program_id(2)
    precise = wg_ref.dtype == F32

    @pl.when((e == 0) & (f == 0))
    def _():
        x = x_ref[...]
        xn = _rms_norm(x, g_ref[...])
        o_ref[...] = x
        if precise or moe:
            xn_hi, xn_lo = _split(xn)
            xhi_sc[...] = xn_hi
            if precise:
                xlo_sc[...] = xn_lo
        else:
            xhi_sc[...] = xn.astype(BF16)
        if moe:
            logits = _mm(xn_hi, xn_lo, wr_ref[...]) + br_ref[...]
            lane = lax.broadcasted_iota(jnp.int32, logits.shape, 1)
            v1 = jnp.max(logits, axis=-1, keepdims=True)
            i1 = jnp.min(jnp.where(logits == v1, lane, LANES), axis=-1, keepdims=True)
            rest = jnp.where(lane == i1, NEG, logits)
            v2 = jnp.max(rest, axis=-1, keepdims=True)
            i2 = jnp.min(jnp.where(rest == v2, lane, LANES), axis=-1, keepdims=True)
            e2 = jnp.exp(v2 - v1)
            g1 = 1.0 / (1.0 + e2)
            g2 = e2 / (1.0 + e2)
            comb_sc[...] = jnp.where(lane == i1, g1, 0.0) + jnp.where(lane == i2, g2, 0.0)

    xn_hi = xhi_sc[...]
    xn_lo = xlo_sc[...] if precise else None
    gate = _mm(xn_hi, xn_lo, wg_ref[0])
    up = _mm(xn_hi, xn_lo, wu_ref[0])
    h = gate * _sigmoid(gate) * up
    if moe:
        lane = lax.broadcasted_iota(jnp.int32, comb_sc.shape, 1)
        h = h * jnp.sum(jnp.where(lane == e, comb_sc[...], 0.0), axis=-1, keepdims=True)
    h_hi, h_lo = _split(h) if precise else (h.astype(BF16), None)
    o_ref[...] += _mm(h_hi, h_lo, wd_ref[0])


def _ffn(x, g, wr, br, wg, wu, wd, *, layer, moe, tm, tf=FFN_TF):
    m = x.shape[0]
    _, n_e, _, d_f = wg.shape
    n_f = d_f // tf
    lo_rows = tm if wg.dtype == F32 else 16
    return pl.pallas_call(
        functools.partial(_ffn_kernel, moe=moe),
        grid=(m // tm, n_e, n_f),
        in_specs=[
            pl.BlockSpec((tm, D_MODEL), lambda i, e, f: (i, 0)),
            pl.BlockSpec((1, D_MODEL), lambda i, e, f: (0, 0)),
            pl.BlockSpec((None, D_MODEL, LANES), lambda i, e, f: (layer, 0, 0)),
            pl.BlockSpec((None, 1, LANES), lambda i, e, f: (layer, 0, 0)),
            pl.BlockSpec((None, 1, D_MODEL, tf), lambda i, e, f: (layer, e, 0, f)),
            pl.BlockSpec((None, 1, D_MODEL, tf), lambda i, e, f: (layer, e, 0, f)),
            pl.BlockSpec((None, 1, tf, D_MODEL), lambda i, e, f: (layer, e, f, 0)),
        ],
        out_specs=pl.BlockSpec((tm, D_MODEL), lambda i, e, f: (i, 0)),
        out_shape=jax.ShapeDtypeStruct((m, D_MODEL), F32),
        scratch_shapes=[pltpu.VMEM((tm, D_MODEL), BF16), pltpu.VMEM((lo_rows, D_MODEL), BF16),
                        pltpu.VMEM((tm, LANES), F32)],
        compiler_params=pltpu.CompilerParams(
            dimension_semantics=("parallel", "arbitrary", "arbitrary"), vmem_limit_bytes=VMEM_LIMIT),
        name=("ffn_moe" if moe else "ffn_dense") + ("_split" if wg.dtype == F32 else ""),
    )(x, g, wr, br, wg, wu, wd)


MOE_TM = 512
MOE_TF = 1024
MOE_ROUTE_TM = 512
MOE_MOVE_TM = 256


def _route_kernel(x_ref, g_ref, wr_ref, br_ref, o_ref):
    xn_hi, xn_lo = _split(_rms_norm(x_ref[...], g_ref[...]))
    logits = _mm(xn_hi, xn_lo, wr_ref[...]) + br_ref[...]
    lane = lax.broadcasted_iota(jnp.int32, logits.shape, 1)
    v1 = jnp.max(logits, axis=-1, keepdims=True)
    i1 = jnp.min(jnp.where(logits == v1, lane, LANES), axis=-1, keepdims=True)
    rest = jnp.where(lane == i1, NEG, logits)
    v2 = jnp.max(rest, axis=-1, keepdims=True)
    i2 = jnp.min(jnp.where(rest == v2, lane, LANES), axis=-1, keepdims=True)
    e2 = jnp.exp(v2 - v1)
    g1 = 1.0 / (1.0 + e2)
    g2 = e2 / (1.0 + e2)
    o_ref[...] = jnp.where(lane == 0, i1.astype(F32),
                           jnp.where(lane == 1, i2.astype(F32),
                                     jnp.where(lane == 2, g1, jnp.where(lane == 3, g2, 0.0))))


def _route(x, g, wr, br, *, layer):
    m = x.shape[0]
    tm = _row_tile(m, MOE_ROUTE_TM)
    return pl.pallas_call(
        _route_kernel,
        grid=(m // tm,),
        in_specs=[
            pl.BlockSpec((tm, D_MODEL), lambda i: (i, 0)),
            pl.BlockSpec((1, D_MODEL), lambda i: (0, 0)),
            pl.BlockSpec((None, D_MODEL, LANES), lambda i: (layer, 0, 0)),
            pl.BlockSpec((None, 1, LANES), lambda i: (layer, 0, 0)),
        ],
        out_specs=pl.BlockSpec((tm, LANES), lambda i: (i, 0)),
        out_shape=jax.ShapeDtypeStruct((m, LANES), F32),
        compiler_params=pltpu.CompilerParams(dimension_semantics=("parallel",), vmem_limit_bytes=VMEM_LIMIT),
        name="moe_route",
    )(x, g, wr, br)


def _moe_plan(route, n_tiles):
    experts = jnp.arange(N_EXPERTS, dtype=jnp.int32)
    e_pair = route[:, :2].astype(jnp.int32).reshape(-1)
    onehot = (e_pair[:, None] == experts[None, :]).astype(jnp.int32)
    csum = jnp.cumsum(onehot, axis=0)
    rank = jnp.sum(csum * onehot, axis=1) - 1
    tiles_e = (csum[-1] + MOE_TM - 1) // MOE_TM
    tend = jnp.cumsum(tiles_e)
    tstart = tend - tiles_e
    pos = (jnp.sum(onehot * tstart[None, :], axis=1) * MOE_TM + rank).astype(jnp.int32)
    n_used = tend[-1:].astype(jnp.int32)
    t_idx = jnp.arange(n_tiles, dtype=jnp.int32)
    tile_e = jnp.sum((t_idx[:, None] >= tend[None, :]).astype(jnp.int32), axis=1)
    last_e = jnp.max(jnp.where(tiles_e > 0, experts, 0))
    tile_e = jnp.minimum(tile_e, last_e).astype(jnp.int32)
    return pos, tile_e, n_used


def _row_copy(src, dst, sem):
    return pltpu.make_async_copy(src, dst, sem)


def _push_kernel(pos_ref, x_ref, g_ref, zeros_ref, xs_ref, xn_sc, sem):
    del zeros_ref
    tm = x_ref.shape[0]
    base = pl.program_id(0) * tm
    xn_sc[...] = _rms_norm(x_ref[...], g_ref[...])

    def start(r, carry):
        for c in range(2):
            dst = pos_ref[2 * (base + r) + c]
            _row_copy(xn_sc.at[pl.ds(r, 1)], xs_ref.at[pl.ds(dst, 1)], sem).start(priority=c)
        return carry

    lax.fori_loop(0, tm, start, 0, unroll=8)

    def wait(r, carry):
        for c in range(2):
            _row_copy(xn_sc.at[pl.ds(r, 1)], xs_ref.at[pl.ds(0, 1)], sem).wait()
        return carry

    lax.fori_loop(0, tm, wait, 0, unroll=8)


def _push(x, g, pos, n_rows):
    m = x.shape[0]
    tm = _row_tile(m, MOE_MOVE_TM)
    zeros = jnp.zeros((n_rows, D_MODEL), F32)
    return pl.pallas_call(
        _push_kernel,
        grid_spec=pltpu.PrefetchScalarGridSpec(
            num_scalar_prefetch=1,
            grid=(m // tm,),
            in_specs=[
                pl.BlockSpec((tm, D_MODEL), lambda i, pos: (i, 0)),
                pl.BlockSpec((1, D_MODEL), lambda i, pos: (0, 0)),
                pl.BlockSpec(memory_space=pl.ANY),
            ],
            out_specs=pl.BlockSpec(memory_space=pl.ANY),
            scratch_shapes=[pltpu.VMEM((tm, D_MODEL), F32), pltpu.SemaphoreType.DMA],
        ),
        out_shape=jax.ShapeDtypeStruct((n_rows, D_MODEL), F32),
        input_output_aliases={3: 0},
        compiler_params=pltpu.CompilerParams(dimension_semantics=("arbitrary",), vmem_limit_bytes=VMEM_LIMIT),
        name="moe_push",
    )(pos, x, g, zeros)


def _experts_kernel(tile_e_ref, n_used_ref, x_ref, wg_ref, wu_ref, wd_ref, o_ref, xb_sc):
    t = pl.program_id(0)
    f = pl.program_id(1)
    used = t < n_used_ref[0]

    @pl.when(f == 0)
    def _():
        xb_sc[...] = x_ref[...].astype(BF16)
        o_ref[...] = jnp.zeros_like(o_ref)

    @pl.when(used)
    def _():
        xn = xb_sc[...]
        gate = _dot(xn, wg_ref[0])
        up = _dot(xn, wu_ref[0])
        h = gate * _sigmoid(gate) * up
        o_ref[...] += _dot(h.astype(BF16), wd_ref[0])


def _experts(xs, tile_e, n_used, wg, wu, wd, *, layer):
    n_rows = xs.shape[0]
    n_f = wg.shape[3] // MOE_TF
    return pl.pallas_call(
        _experts_kernel,
        grid_spec=pltpu.PrefetchScalarGridSpec(
            num_scalar_prefetch=2,
            grid=(n_rows // MOE_TM, n_f),
            in_specs=[
                pl.BlockSpec((MOE_TM, D_MODEL), lambda t, f, te, nu: (t, 0)),
                pl.BlockSpec((None, 1, D_MODEL, MOE_TF), lambda t, f, te, nu: (layer, te[t], 0, f)),
                pl.BlockSpec((None, 1, D_MODEL, MOE_TF), lambda t, f, te, nu: (layer, te[t], 0, f)),
                pl.BlockSpec((None, 1, MOE_TF, D_MODEL), lambda t, f, te, nu: (layer, te[t], f, 0)),
            ],
            out_specs=pl.BlockSpec((MOE_TM, D_MODEL), lambda t, f, te, nu: (t, 0)),
            scratch_shapes=[pltpu.VMEM((MOE_TM, D_MODEL), BF16)],
        ),
        out_shape=jax.ShapeDtypeStruct((n_rows, D_MODEL), F32),
        compiler_params=pltpu.CompilerParams(
            dimension_semantics=("arbitrary", "arbitrary"), vmem_limit_bytes=VMEM_LIMIT),
        name="moe_experts",
    )(tile_e, n_used, xs, wg, wu, wd)


def _combine_kernel(pos_ref, x_ref, route_ref, ys_ref, o_ref, y_sc, sem):
    tm = x_ref.shape[0]
    base = pl.program_id(0) * tm

    def start(r, carry):
        for c in range(2):
            src = pos_ref[2 * (base + r) + c]
            _row_copy(ys_ref.at[pl.ds(src, 1)], y_sc.at[c, pl.ds(r, 1)], sem).start(priority=c)
        return carry

    lax.fori_loop(0, tm, start, 0, unroll=8)

    def wait(r, carry):
        for c in range(2):
            _row_copy(ys_ref.at[pl.ds(0, 1)], y_sc.at[c, pl.ds(r, 1)], sem).wait()
        return carry

    lax.fori_loop(0, tm, wait, 0, unroll=8)
    o_ref[...] = x_ref[...] + route_ref[:, 2:3] * y_sc[0] + route_ref[:, 3:4] * y_sc[1]


def _combine(x, route, pos, ys):
    m = x.shape[0]
    tm = _row_tile(m, MOE_MOVE_TM)
    return pl.pallas_call(
        _combine_kernel,
        grid_spec=pltpu.PrefetchScalarGridSpec(
            num_scalar_prefetch=1,
            grid=(m // tm,),
            in_specs=[
                pl.BlockSpec((tm, D_MODEL), lambda i, pos: (i, 0)),
                pl.BlockSpec((tm, LANES), lambda i, pos: (i, 0)),
                pl.BlockSpec(memory_space=pl.ANY),
            ],
            out_specs=pl.BlockSpec((tm, D_MODEL), lambda i, pos: (i, 0)),
            scratch_shapes=[pltpu.VMEM((2, tm, D_MODEL), F32), pltpu.SemaphoreType.DMA],
        ),
        out_shape=jax.ShapeDtypeStruct((m, D_MODEL), F32),
        compiler_params=pltpu.CompilerParams(dimension_semantics=("arbitrary",), vmem_limit_bytes=VMEM_LIMIT),
        name="moe_combine",
    )(pos, x, route, ys)


def _moe_routed(x, g, wr, br, wg, wu, wd, *, layer):
    m = x.shape[0]
    n_tiles = (2 * m) // MOE_TM + N_EXPERTS
    route = _route(x, g, wr, br, layer=layer)
    pos, tile_e, n_used = _moe_plan(route, n_tiles)
    xs = _push(x, g, pos, n_tiles * MOE_TM)
    ys = _experts(xs, tile_e, n_used, wg, wu, wd, layer=layer)
    return _combine(x, route, pos, ys)


def _final_norm_kernel(x_ref, g_ref, o_ref):
    o_ref[...] = _rms_norm(x_ref[...], g_ref[...])


def _final_norm(x, g, *, tm):
    m = x.shape[0]
    return pl.pallas_call(
        _final_norm_kernel,
        grid=(m // tm,),
        in_specs=[pl.BlockSpec((tm, D_MODEL), lambda i: (i, 0)), pl.BlockSpec((1, D_MODEL), lambda i: (0, 0))],
        out_specs=pl.BlockSpec((tm, D_MODEL), lambda i: (i, 0)),
        out_shape=jax.ShapeDtypeStruct((m, D_MODEL), F32),
        compiler_params=pltpu.CompilerParams(dimension_semantics=("parallel",), vmem_limit_bytes=VMEM_LIMIT),
        name="final_norm",
    )(x, g)


def _rope_tables(pos):
    half = HEAD_DIM // 2
    inv = ROPE_THETA ** (-2.0 * jnp.arange(half, dtype=F32) / HEAD_DIM)
    ang = pos.astype(F32)[:, None] * inv[None, :]
    cos, sin = jnp.cos(ang), jnp.sin(ang)
    reps = LANES // HEAD_DIM
    return (jnp.tile(jnp.concatenate([cos, cos], axis=-1), (1, reps)),
            jnp.tile(jnp.concatenate([-sin, sin], axis=-1), (1, reps)))


def _row_tile(m, cap):
    t = cap
    while m % t:
        t //= 2
    return t


def _put_tail(main, tail, batch, seq):
    c = main.shape[1]
    upd = tail.reshape(batch, BLOCK, c).astype(main.dtype)
    return lax.dynamic_update_slice(main.reshape(batch, seq, c), upd, (0, seq - BLOCK, 0)).reshape(batch * seq, c)


def kernel(x_prompt, x_sample, cache_k, cache_v, state_pool, norm_mix, norm_ffn, norm_final, w_in, attn_sinks,
           w_pool_map, pool_scale, w_out, w_ffn_gate, w_ffn_up, w_ffn_down, w_router, b_router, w_exp_gate,
           w_exp_up, w_exp_down):
    batch, seq, _ = x_prompt.shape
    dec_batch, dec_seq, _ = x_sample.shape
    depth = w_in.shape[0]
    assert seq % BLOCK == 0 and dec_batch % SAMPLE_DB == 0 and dec_seq <= POOL_HIST

    w_in_b = w_in.astype(BF16)
    w_pool_b = w_pool_map.astype(BF16)
    w_out_b = w_out.astype(BF16)
    w_fg, w_fu, w_fd = (w[:, None] for w in (w_ffn_gate, w_ffn_up, w_ffn_down))
    w_fg_b, w_fu_b, w_fd_b = (w.astype(BF16) for w in (w_fg, w_fu, w_fd))
    w_eg_b, w_eu_b, w_ed_b = (w.astype(BF16) for w in (w_exp_gate, w_exp_up, w_exp_down))
    w_r = jnp.pad(w_router.astype(F32), ((0, 0), (0, 0), (0, LANES - N_EXPERTS)))
    b_r = jnp.pad(b_router.astype(F32), ((0, 0), (0, LANES - N_EXPERTS)), constant_values=NEG)[:, None]
    zero_wr = jnp.zeros((w_fg.shape[0], D_MODEL, LANES), F32)
    zero_br = jnp.zeros((w_fg.shape[0], 1, LANES), F32)

    pos_p = jnp.arange(seq, dtype=jnp.int32)
    pos_s = jnp.tile(PAST_LEN + jnp.arange(dec_seq, dtype=jnp.int32), dec_batch)
    cos_p, sin_p = _rope_tables(pos_p)
    cos_s, sin_s = _rope_tables(pos_s)
    cos_x, sin_x = _rope_tables(jnp.concatenate([jnp.tile(pos_p[seq - BLOCK:], batch), pos_s]))

    m_p, m_s, m_t = batch * seq, dec_batch * dec_seq, batch * BLOCK
    tm_in_p = _row_tile(seq, IN_TM)
    tm_ffn_p = _row_tile(m_p, FFN_TM)

    xp = x_prompt.reshape(m_p, D_MODEL)
    xs = x_sample.reshape(m_s, D_MODEL)
    ck = cache_k.reshape(depth, dec_batch, WINDOW, KV_DIM)
    cv = cache_v.reshape(depth, dec_batch, WINDOW, KV_DIM)

    n_split = min(2, depth)
    xx = jnp.concatenate([x_prompt[:, seq - BLOCK:].reshape(m_t, D_MODEL), xs], axis=0)

    kp_l, vp_l, pp_l, ks_l, vs_l, ps_l = [], [], [], [], [], []
    for l in range(depth):
        g_mix = norm_mix[l][None]
        g_ffn = norm_ffn[l][None]
        pscale = pool_scale[l][None]
        sinks = attn_sinks[l]
        split = l < n_split

        q, k, v, u, ga, gp = _inproj(xp, g_mix, w_in_b, cos_p, sin_p, layer=l, tm=tm_in_p)
        if split:
            qx, kx, vx, ux, gax, gpx = _inproj(xx, g_mix, w_in, cos_x, sin_x, layer=l, tm=m_t + m_s)
            k, v, u = (_put_tail(a, b[:m_t], batch, seq) for a, b in ((k, kx), (v, vx), (u, ux)))
        xp = _mixer_prompt(q, k, v, u, ga, gp, xp, sinks, w_pool_b, pscale, w_out_b, layer=l, batch=batch, seq=seq)
        if split:
            mix_t = _mixer_prompt_tail(qx[:m_t], k, v, u, gax[:m_t], gpx[:m_t], sinks, w_pool_map, pscale,
                                       layer=l, batch=batch, seq=seq)
            mix_s, nk, nv, npool = _mixer_sample(qx[m_t:], kx[m_t:], vx[m_t:], ux[m_t:], gax[m_t:], gpx[m_t:],
                                                 ck, cv, state_pool, sinks, w_pool_map, pscale,
                                                 layer=l, dec_seq=dec_seq)
            xx = _proj_residual(jnp.concatenate([mix_t, mix_s], axis=0), xx, w_out, layer=l)
            xp = _put_tail(xp, xx[:m_t], batch, seq)
        else:
            qs, ks, vs, us, gas, gps = _inproj(xs, g_mix, w_in_b, cos_s, sin_s, layer=l, tm=m_s)
            mix_s, nk, nv, npool = _mixer_sample(qs, ks, vs, us, gas, gps, ck, cv, state_pool, sinks,
                                                 w_pool_b, pscale, layer=l, dec_seq=dec_seq)
            xs = _proj_residual(mix_s, xs, w_out_b, layer=l)
        kp_l.append(k.reshape(batch, seq, N_KV_HEADS, HEAD_DIM)[:, seq - WINDOW:])
        vp_l.append(v.reshape(batch, seq, N_KV_HEADS, HEAD_DIM)[:, seq - WINDOW:])
        pp_l.append(u.reshape(batch, seq, POOL_DIM)[:, seq - POOL_HIST:])
        ks_l.append(nk.reshape(dec_batch, WINDOW, N_KV_HEADS, HEAD_DIM))
        vs_l.append(nv.reshape(dec_batch, WINDOW, N_KV_HEADS, HEAD_DIM))
        ps_l.append(npool)

        i = l // 2
        if l % 2 == 0:
            xp = _ffn(xp, g_ffn, zero_wr, zero_br, w_fg_b, w_fu_b, w_fd_b, layer=i, moe=False, tm=tm_ffn_p)
            if split and l + 1 < n_split:
                xx = _ffn(xx, g_ffn, zero_wr, zero_br, w_fg, w_fu, w_fd, layer=i, moe=False, tm=m_t + m_s)
                xp = _put_tail(xp, xx[:m_t], batch, seq)
            else:
                if split:
                    xs = xx[m_t:]
                xs = _ffn(xs, g_ffn, zero_wr, zero_br, w_fg_b, w_fu_b, w_fd_b, layer=i, moe=False, tm=m_s)
        else:
            if split:
                xs = xx[m_t:]
            xp = _moe_routed(xp, g_ffn, w_r, b_r, w_eg_b, w_eu_b, w_ed_b, layer=i)
            xs = _ffn(xs, g_ffn, w_r, b_r, w_eg_b, w_eu_b, w_ed_b, layer=i, moe=True, tm=m_s, tf=FFN_TF_SAMPLE)

    g_fin = norm_final[None]
    y_p = _final_norm(xp, g_fin, tm=tm_ffn_p).reshape(batch, seq, D_MODEL)
    y_s = _final_norm(xs, g_fin, tm=m_s).reshape(dec_batch, dec_seq, D_MODEL)
    return (y_p, y_s, jnp.stack(kp_l), jnp.stack(vp_l), jnp.stack(pp_l),
            jnp.stack(ks_l), jnp.stack(vs_l), jnp.stack(ps_l))
```

```python
import functools

import jax
import jax.numpy as jnp
from jax import lax
from jax.experimental import pallas as pl
from jax.experimental.pallas import tpu as pltpu

D_MODEL = 2048
HEAD_DIM = 64
N_HEADS = 32
N_KV_HEADS = 4
GROUP = N_HEADS // N_KV_HEADS
KV_DIM = N_KV_HEADS * HEAD_DIM
WINDOW = 128
BLOCK = 128
ROPE_THETA = 10000.0
PAST_LEN = 16384
POOL_WINDOWS = (2, 4, 8, 16)
POOL_GROUP_DIM = 256
POOL_DIM = 1024
POOL_OUT_GROUP = 512
POOL_HIST = 15
N_EXPERTS = 8
EPS = 1e-5
NEG = -1e30

BF16 = jnp.bfloat16
F32 = jnp.float32

LANES = 128
VMEM_LIMIT = 52 * 1024 * 1024

IN_TN = 512
IN_TM = 1024
IN_Q_TILES = D_MODEL // IN_TN
IN_KV_TILE = IN_Q_TILES
IN_U_START = IN_KV_TILE + 1
IN_U_TILES = POOL_DIM // IN_TN
IN_GA_START = IN_U_START + IN_U_TILES
IN_G_TILES = D_MODEL // IN_TN
IN_GP_START = IN_GA_START + IN_G_TILES
IN_TILES = IN_GP_START + IN_G_TILES


def _sigmoid(x):
    return 1.0 / (1.0 + jnp.exp(-x))


def _split(a):
    hi = a.astype(BF16)
    lo = (a - hi.astype(F32)).astype(BF16)
    return hi, lo


def _dot(a, b):
    return jnp.dot(a, b, preferred_element_type=F32)


def _dot_nt(a, b):
    return lax.dot_general(a, b, (((1,), (1,)), ((), ())), preferred_element_type=F32)


def _mm(a_hi, a_lo, w):
    if w.dtype == BF16:
        return _dot(a_hi, w)
    w_hi, w_lo = _split(w)
    return _dot(a_hi, w_hi) + (_dot(a_lo, w_hi) + _dot(a_hi, w_lo))


def _qk(q, k_hi, k_lo):
    if q.dtype == BF16:
        return _dot_nt(q, k_hi)
    q_hi, q_lo = _split(q)
    return _dot_nt(q_hi, k_hi) + (_dot_nt(q_lo, k_hi) + _dot_nt(q_hi, k_lo))


def _pv(p, v_hi, v_lo):
    if v_lo is None:
        return _dot(p.astype(BF16), v_hi)
    p_hi, p_lo = _split(p)
    return _dot(p_hi, v_hi) + (_dot(p_lo, v_hi) + _dot(p_hi, v_lo))


def _kv_operands(t, precise):
    return _split(t) if precise else (t.astype(BF16), None)


def _cols(t, c):
    return None if t is None else t[:, c]


def _pool_map(pm, w_ref, gi):
    pm_hi, pm_lo = _split(pm) if w_ref.dtype == F32 else (pm.astype(BF16), None)
    return _mm(pm_hi, pm_lo, w_ref[gi])


def _rms_norm(x, g):
    ms = jnp.mean(x * x, axis=-1, keepdims=True)
    return x * lax.rsqrt(ms + EPS) * g


def _rope(t, cos, sin_signed):
    width = t.shape[1]
    lane = lax.broadcasted_iota(jnp.int32, t.shape, 1)
    first_half = (lane % HEAD_DIM) < (HEAD_DIM // 2)
    rot = jnp.where(first_half, pltpu.roll(t, width - HEAD_DIM // 2, 1), pltpu.roll(t, HEAD_DIM // 2, 1))
    reps = width // cos.shape[1]
    return t * jnp.tile(cos, (1, reps)) + rot * jnp.tile(sin_signed, (1, reps))


def _inproj_kernel(x_ref, g_ref, w_ref, cos_ref, sin_ref, q_ref, k_ref, v_ref, u_ref, ga_ref, gp_ref,
                   xhi_sc, xlo_sc):
    j = pl.program_id(1)
    precise = w_ref.dtype == F32

    @pl.when(j == 0)
    def _():
        xn = _rms_norm(x_ref[...], g_ref[...])
        if precise:
            xhi_sc[...], xlo_sc[...] = _split(xn)
        else:
            xhi_sc[...] = xn.astype(BF16)

    def z():
        return _mm(xhi_sc[...], xlo_sc[...] if precise else None, w_ref[...])

    @pl.when(j < IN_Q_TILES)
    def _():
        q_ref[...] = (_rope(z(), cos_ref[...], sin_ref[...]) * (HEAD_DIM ** -0.5)).astype(q_ref.dtype)

    @pl.when(j == IN_KV_TILE)
    def _():
        kv = z()
        k_ref[...] = _rope(kv[:, :KV_DIM], cos_ref[...], sin_ref[...])
        v_ref[...] = kv[:, KV_DIM:]

    @pl.when((j >= IN_U_START) & (j < IN_GA_START))
    def _():
        u_ref[...] = z()

    @pl.when((j >= IN_GA_START) & (j < IN_GP_START))
    def _():
        ga_ref[...] = z()

    @pl.when(j >= IN_GP_START)
    def _():
        gp_ref[...] = z()


def _inproj(x, g, w, cos, sin, *, layer, tm):
    m = x.shape[0]
    pos_blocks = cos.shape[0] // tm
    grid = (m // tm, IN_TILES)

    def clip(j, lo, n):
        return jnp.clip(j - lo, 0, n - 1)

    lo_rows = tm if w.dtype == F32 else 16
    return pl.pallas_call(
        _inproj_kernel,
        grid=grid,
        in_specs=[
            pl.BlockSpec((tm, D_MODEL), lambda i, j: (i, 0), pipeline_mode=pl.Buffered(1)),
            pl.BlockSpec((1, D_MODEL), lambda i, j: (0, 0)),
            pl.BlockSpec((None, D_MODEL, IN_TN), lambda i, j: (layer, 0, j)),
            pl.BlockSpec((tm, LANES), lambda i, j: (i % pos_blocks, 0)),
            pl.BlockSpec((tm, LANES), lambda i, j: (i % pos_blocks, 0)),
        ],
        out_specs=[
            pl.BlockSpec((tm, IN_TN), lambda i, j: (i, clip(j, 0, IN_Q_TILES))),
            pl.BlockSpec((tm, KV_DIM), lambda i, j: (i, 0)),
            pl.BlockSpec((tm, KV_DIM), lambda i, j: (i, 0)),
            pl.BlockSpec((tm, IN_TN), lambda i, j: (i, clip(j, IN_U_START, IN_U_TILES))),
            pl.BlockSpec((tm, IN_TN), lambda i, j: (i, clip(j, IN_GA_START, IN_G_TILES))),
            pl.BlockSpec((tm, IN_TN), lambda i, j: (i, clip(j, IN_GP_START, IN_G_TILES))),
        ],
        out_shape=[
            jax.ShapeDtypeStruct((m, D_MODEL), w.dtype),
            jax.ShapeDtypeStruct((m, KV_DIM), F32),
            jax.ShapeDtypeStruct((m, KV_DIM), F32),
            jax.ShapeDtypeStruct((m, POOL_DIM), F32),
            jax.ShapeDtypeStruct((m, D_MODEL), F32),
            jax.ShapeDtypeStruct((m, D_MODEL), F32),
        ],
        scratch_shapes=[pltpu.VMEM((tm, D_MODEL), BF16), pltpu.VMEM((lo_rows, D_MODEL), BF16)],
        compiler_params=pltpu.CompilerParams(
            dimension_semantics=("parallel", "arbitrary"), vmem_limit_bytes=VMEM_LIMIT),
        name="inproj_split" if w.dtype == F32 else "inproj",
    )(x, g, w, cos, sin)


def _mixer_prompt_kernel(sinks_ref, q_ref, kc_ref, kp_ref, vc_ref, vp_ref, uc_ref, up_ref, ga_ref, gp_ref,
                         wpool_ref, pscale_ref, *rest, first_block, n_blocks):
    precise = q_ref.dtype == F32
    if precise:
        mix_ref, ext_sc = rest
        n = first_block + pl.program_id(1)
    else:
        x_ref, *wout_refs, o_ref, ext_sc, mix_ref, prev_sc = rest
        step = pl.program_id(1)
        n = jnp.minimum(step, n_blocks - 1)

        @pl.when((pl.program_id(0) == 0) & (step == 0))
        def _():
            mix_ref[...] = jnp.zeros_like(mix_ref)

        prev_sc[...] = mix_ref[...]
    not_first = n > 0

    kb = jnp.concatenate([kp_ref[...], kc_ref[...]], axis=0)
    vb = jnp.concatenate([vp_ref[...], vc_ref[...]], axis=0)
    qi = lax.broadcasted_iota(jnp.int32, (BLOCK, 2 * BLOCK), 0)
    kj = lax.broadcasted_iota(jnp.int32, (BLOCK, 2 * BLOCK), 1)
    dist = qi + BLOCK - kj
    mask = (dist >= 0) & (dist <= WINDOW) & ((kj >= BLOCK) | not_first)
    lane = lax.broadcasted_iota(jnp.int32, (BLOCK, 2 * HEAD_DIM), 1)
    zeros_kv = jnp.zeros((2 * BLOCK, HEAD_DIM), F32)
    sum_cols = [(lax.broadcasted_iota(jnp.int32, (2 * BLOCK, 2 * HEAD_DIM), 1) == c).astype(F32) for c in range(2)]

    ext_sc[0:16, :] = jnp.where(not_first, up_ref[BLOCK - 16:BLOCK, :], 0.0)
    ext_sc[16:16 + BLOCK, :] = uc_ref[...]
    pos = n * BLOCK + lax.broadcasted_iota(jnp.int32, (BLOCK, 1), 0)

    for kh in range(N_KV_HEADS):
        w = POOL_WINDOWS[kh]
        pcs = slice(kh * POOL_GROUP_DIM, (kh + 1) * POOL_GROUP_DIM)
        cur = ext_sc[16:16 + BLOCK, pcs]
        acc = cur
        for d in range(1, w):
            acc = acc + ext_sc[16 - d:16 - d + BLOCK, pcs]
        inv_cnt = 1.0 / jnp.minimum(pos + 1, w).astype(F32)
        pm = acc * inv_cnt - cur
        ocs = slice(kh * POOL_OUT_GROUP, (kh + 1) * POOL_OUT_GROUP)
        pg = _pool_map(pm, wpool_ref, kh) * pscale_ref[:, ocs]

        hs = slice(kh * HEAD_DIM, (kh + 1) * HEAD_DIM)
        k_h, v_h = kb[:, hs], vb[:, hs]
        k_op = _kv_operands(jnp.concatenate([jnp.concatenate([k_h, zeros_kv], axis=1),
                                             jnp.concatenate([zeros_kv, k_h], axis=1)], axis=0), precise)
        v_op = _kv_operands(jnp.concatenate([jnp.concatenate([v_h, zeros_kv, sum_cols[0]], axis=1),
                                             jnp.concatenate([zeros_kv, v_h, sum_cols[1]], axis=1)], axis=0), precise)
        pairs = []
        for pair in range(GROUP // 2):
            slab = (kh * GROUP // 2 + pair) * 2 * HEAD_DIM
            s2 = _qk(q_ref[:, slab:slab + 2 * HEAD_DIM], *k_op)
            probs, sink_terms = [], []
            for c in range(2):
                s = jnp.where(mask, s2[:, c * 2 * BLOCK:(c + 1) * 2 * BLOCK], NEG)
                sink = sinks_ref[kh * GROUP + 2 * pair + c]
                m = jnp.maximum(jnp.max(s, axis=-1, keepdims=True), sink)
                probs.append(jnp.exp(s - m))
                sink_terms.append(jnp.exp(sink - m))
            out = _pv(jnp.concatenate(probs, axis=1), *v_op)
            inv = [1.0 / (out[:, 2 * HEAD_DIM + c:2 * HEAD_DIM + c + 1] + sink_terms[c]) for c in range(2)]
            pairs.append(out[:, :2 * HEAD_DIM] * jnp.where(lane < HEAD_DIM, inv[0], inv[1]))
        a = jnp.concatenate(pairs, axis=1)
        mix = _sigmoid(ga_ref[:, ocs]) * a + _sigmoid(gp_ref[:, ocs]) * pg
        mix_ref[:, ocs] = mix.astype(mix_ref.dtype)
        if not precise:
            o_ref[:, ocs] = x_ref[:, ocs] + _dot(prev_sc[...], wout_refs[kh][...])


def _mixer_prompt(q, k, v, u, ga, gp, x, sinks, wpool, pscale, wout, *, layer, batch, seq):
    nb = seq // BLOCK
    m = batch * seq

    def cur(b, s):
        return (b * nb + jnp.minimum(s, nb - 1), 0)

    def prev(b, s):
        return (jnp.maximum(b * nb + jnp.minimum(s, nb - 1) - 1, 0), 0)

    def proj(b, s):
        return (b * nb + jnp.maximum(s - 1, 0), 0)

    return pl.pallas_call(
        functools.partial(_mixer_prompt_kernel, first_block=0, n_blocks=nb),
        grid=(batch, nb + 1),
        in_specs=[
            pl.BlockSpec(memory_space=pltpu.SMEM),
            pl.BlockSpec((BLOCK, D_MODEL), cur),
            pl.BlockSpec((BLOCK, KV_DIM), cur),
            pl.BlockSpec((BLOCK, KV_DIM), prev),
            pl.BlockSpec((BLOCK, KV_DIM), cur),
            pl.BlockSpec((BLOCK, KV_DIM), prev),
            pl.BlockSpec((BLOCK, POOL_DIM), cur),
            pl.BlockSpec((BLOCK, POOL_DIM), prev),
            pl.BlockSpec((BLOCK, D_MODEL), cur),
            pl.BlockSpec((BLOCK, D_MODEL), cur),
            pl.BlockSpec((None, N_KV_HEADS, POOL_GROUP_DIM, POOL_OUT_GROUP), lambda b, n: (layer, 0, 0, 0)),
            pl.BlockSpec((1, D_MODEL), lambda b, n: (0, 0)),
            pl.BlockSpec((BLOCK, D_MODEL), proj),
            *[pl.BlockSpec((None, D_MODEL, POOL_OUT_GROUP), functools.partial(lambda b, n, c: (layer, 0, c), c=c))
              for c in range(N_KV_HEADS)],
        ],
        out_specs=pl.BlockSpec((BLOCK, D_MODEL), proj),
        out_shape=jax.ShapeDtypeStruct((m, D_MODEL), F32),
        scratch_shapes=[pltpu.VMEM((16 + BLOCK, POOL_DIM), F32), pltpu.VMEM((BLOCK, D_MODEL), BF16),
                        pltpu.VMEM((BLOCK, D_MODEL), BF16)],
        compiler_params=pltpu.CompilerParams(
            dimension_semantics=("arbitrary", "arbitrary"), vmem_limit_bytes=VMEM_LIMIT),
        name="mixer_prompt",
    )(sinks, q, k, k, v, v, u, u, ga, gp, wpool, pscale, x, *([wout] * N_KV_HEADS))


def _mixer_prompt_tail(q_t, k, v, u, ga_t, gp_t, sinks, wpool, pscale, *, layer, batch, seq):
    nb = seq // BLOCK

    def tail(b, n):
        return (b, 0)

    def cur(b, n):
        return (b * nb + nb - 1, 0)

    def prev(b, n):
        return (jnp.maximum(b * nb + nb - 2, 0), 0)

    return pl.pallas_call(
        functools.partial(_mixer_prompt_kernel, first_block=nb - 1, n_blocks=nb),
        grid=(batch, 1),
        in_specs=[
            pl.BlockSpec(memory_space=pltpu.SMEM),
            pl.BlockSpec((BLOCK, D_MODEL), tail),
            pl.BlockSpec((BLOCK, KV_DIM), cur),
            pl.BlockSpec((BLOCK, KV_DIM), prev),
            pl.BlockSpec((BLOCK, KV_DIM), cur),
            pl.BlockSpec((BLOCK, KV_DIM), prev),
            pl.BlockSpec((BLOCK, POOL_DIM), cur),
            pl.BlockSpec((BLOCK, POOL_DIM), prev),
            pl.BlockSpec((BLOCK, D_MODEL), tail),
            pl.BlockSpec((BLOCK, D_MODEL), tail),
            pl.BlockSpec((None, N_KV_HEADS, POOL_GROUP_DIM, POOL_OUT_GROUP), lambda b, n: (layer, 0, 0, 0)),
            pl.BlockSpec((1, D_MODEL), lambda b, n: (0, 0)),
        ],
        out_specs=pl.BlockSpec((BLOCK, D_MODEL), tail),
        out_shape=jax.ShapeDtypeStruct((batch * BLOCK, D_MODEL), F32),
        scratch_shapes=[pltpu.VMEM((16 + BLOCK, POOL_DIM), F32)],
        compiler_params=pltpu.CompilerParams(
            dimension_semantics=("parallel", "arbitrary"), vmem_limit_bytes=VMEM_LIMIT),
        name="mixer_prompt_tail",
    )(sinks, q_t, k, k, v, v, u, u, ga_t, gp_t, wpool, pscale)


SAMPLE_DB = 8


def _mixer_sample_kernel(sinks_ref, q_ref, kn_ref, vn_ref, u_ref, ga_ref, gp_ref, ck_ref, cv_ref, pool_ref,
                         wpool_ref, pscale_ref, mix_ref, nk_ref, nv_ref, npool_ref, ext_sc, pm_sc, a_sc,
                         *, dec_seq):
    t_new = dec_seq
    rows = SAMPLE_DB * t_new
    precise = q_ref.dtype == F32
    kc_hi, kc_lo = _kv_operands(ck_ref[...].reshape(SAMPLE_DB * WINDOW, KV_DIM), precise)
    vc_hi, vc_lo = _kv_operands(cv_ref[...].reshape(SAMPLE_DB * WINDOW, KV_DIM), precise)
    kn_hi, kn_lo = _kv_operands(kn_ref[...], precise)
    vn_hi, vn_lo = _kv_operands(vn_ref[...], precise)

    rho = lax.broadcasted_iota(jnp.int32, (GROUP * rows, 1), 0)
    q_db = (rho % rows) // t_new
    q_t = rho % t_new
    q_g = rho // rows
    colc = lax.broadcasted_iota(jnp.int32, (1, SAMPLE_DB * WINDOW), 1)
    mask_c = (q_db == colc // WINDOW) & ((colc % WINDOW) >= q_t)
    coln = lax.broadcasted_iota(jnp.int32, (1, rows), 1)
    mask_n = (q_db == coln // t_new) & ((coln % t_new) <= q_t)

    for d in range(SAMPLE_DB):
        rs = slice(d * t_new, (d + 1) * t_new)
        ext_sc[d, 0:POOL_HIST, :] = pool_ref[d]
        ext_sc[d, POOL_HIST:POOL_HIST + t_new, :] = u_ref[rs, :]
        npool_ref[d] = ext_sc[d, t_new:t_new + POOL_HIST, :]
        nk_ref[d, 0:WINDOW - t_new, :] = ck_ref[d, t_new:WINDOW, :]
        nk_ref[d, WINDOW - t_new:WINDOW, :] = kn_ref[rs, :]
        nv_ref[d, 0:WINDOW - t_new, :] = cv_ref[d, t_new:WINDOW, :]
        nv_ref[d, WINDOW - t_new:WINDOW, :] = vn_ref[rs, :]
        for gi, w in enumerate(POOL_WINDOWS):
            pcs = slice(gi * POOL_GROUP_DIM, (gi + 1) * POOL_GROUP_DIM)
            cur = ext_sc[d, POOL_HIST:POOL_HIST + t_new, pcs]
            acc = cur
            for j in range(1, w):
                acc = acc + ext_sc[d, POOL_HIST - j:POOL_HIST - j + t_new, pcs]
            pm_sc[rs, pcs] = acc / float(w) - cur

    for kh in range(N_KV_HEADS):
        hs = slice(kh * HEAD_DIM, (kh + 1) * HEAD_DIM)
        qs = jnp.concatenate(
            [q_ref[:, (kh * GROUP + g) * HEAD_DIM:(kh * GROUP + g + 1) * HEAD_DIM] for g in range(GROUP)], axis=0)
        sc = jnp.where(mask_c, _qk(qs, kc_hi[:, hs], _cols(kc_lo, hs)), NEG)
        sn = jnp.where(mask_n, _qk(qs, kn_hi[:, hs], _cols(kn_lo, hs)), NEG)
        sink = jnp.zeros((GROUP * rows, 1), F32)
        for g in range(GROUP):
            sink = jnp.where(q_g == g, sinks_ref[kh * GROUP + g], sink)
        m = jnp.maximum(jnp.maximum(jnp.max(sc, axis=-1, keepdims=True), jnp.max(sn, axis=-1, keepdims=True)), sink)
        pc = jnp.exp(sc - m)
        pn = jnp.exp(sn - m)
        denom = jnp.sum(pc, axis=-1, keepdims=True) + jnp.sum(pn, axis=-1, keepdims=True) + jnp.exp(sink - m)
        o = (_pv(pc, vc_hi[:, hs], _cols(vc_lo, hs)) + _pv(pn, vn_hi[:, hs], _cols(vn_lo, hs))) / denom
        for g in range(GROUP):
            h = kh * GROUP + g
            a_sc[:, h * HEAD_DIM:(h + 1) * HEAD_DIM] = o[g * rows:(g + 1) * rows, :]

    for gi in range(N_KV_HEADS):
        pcs = slice(gi * POOL_GROUP_DIM, (gi + 1) * POOL_GROUP_DIM)
        ocs = slice(gi * POOL_OUT_GROUP, (gi + 1) * POOL_OUT_GROUP)
        pg = _pool_map(pm_sc[:, pcs], wpool_ref, gi) * pscale_ref[:, ocs]
        mix_ref[:, ocs] = _sigmoid(ga_ref[:, ocs]) * a_sc[:, ocs] + _sigmoid(gp_ref[:, ocs]) * pg


def _mixer_sample(q, kn, vn, u, ga, gp, ck, cv, pool, sinks, wpool, pscale, *, layer, dec_seq):
    db = ck.shape[1]
    rows = SAMPLE_DB * dec_seq
    m = db * dec_seq
    row_blk = lambda i: (i, 0)
    db_blk = lambda i: (i, 0, 0)
    db_layer_blk = lambda i: (layer, i, 0, 0)
    return pl.pallas_call(
        functools.partial(_mixer_sample_kernel, dec_seq=dec_seq),
        grid=(db // SAMPLE_DB,),
        in_specs=[
            pl.BlockSpec(memory_space=pltpu.SMEM),
            pl.BlockSpec((rows, D_MODEL), row_blk),
            pl.BlockSpec((rows, KV_DIM), row_blk),
            pl.BlockSpec((rows, KV_DIM), row_blk),
            pl.BlockSpec((rows, POOL_DIM), row_blk),
            pl.BlockSpec((rows, D_MODEL), row_blk),
            pl.BlockSpec((rows, D_MODEL), row_blk),
            pl.BlockSpec((None, SAMPLE_DB, WINDOW, KV_DIM), db_layer_blk),
            pl.BlockSpec((None, SAMPLE_DB, WINDOW, KV_DIM), db_layer_blk),
            pl.BlockSpec((None, SAMPLE_DB, POOL_HIST, POOL_DIM), db_layer_blk),
            pl.BlockSpec((None, N_KV_HEADS, POOL_GROUP_DIM, POOL_OUT_GROUP), lambda i: (layer, 0, 0, 0)),
            pl.BlockSpec((1, D_MODEL), lambda i: (0, 0)),
        ],
        out_specs=[
            pl.BlockSpec((rows, D_MODEL), row_blk),
            pl.BlockSpec((SAMPLE_DB, WINDOW, KV_DIM), db_blk),
            pl.BlockSpec((SAMPLE_DB, WINDOW, KV_DIM), db_blk),
            pl.BlockSpec((SAMPLE_DB, POOL_HIST, POOL_DIM), db_blk),
        ],
        out_shape=[
            jax.ShapeDtypeStruct((m, D_MODEL), F32),
            jax.ShapeDtypeStruct((db, WINDOW, KV_DIM), F32),
            jax.ShapeDtypeStruct((db, WINDOW, KV_DIM), F32),
            jax.ShapeDtypeStruct((db, POOL_HIST, POOL_DIM), F32),
        ],
        scratch_shapes=[
            pltpu.VMEM((SAMPLE_DB, 24, POOL_DIM), F32),
            pltpu.VMEM((rows, POOL_DIM), F32),
            pltpu.VMEM((rows, D_MODEL), F32),
        ],
        compiler_params=pltpu.CompilerParams(
            dimension_semantics=("arbitrary",), vmem_limit_bytes=VMEM_LIMIT),
        name="mixer_sample_split" if q.dtype == F32 else "mixer_sample",
    )(sinks, q, kn, vn, u, ga, gp, ck, cv, pool, wpool, pscale)


PROJ_TN = 512


def _proj_residual_kernel(a_ref, x_ref, w_ref, o_ref):
    a = a_ref[...]
    a_hi, a_lo = _split(a) if w_ref.dtype == F32 else (a.astype(BF16), None)
    o_ref[...] = x_ref[...] + _mm(a_hi, a_lo, w_ref[...])


def _proj_residual(a, x, w, *, layer):
    m = x.shape[0]
    return pl.pallas_call(
        _proj_residual_kernel,
        grid=(D_MODEL // PROJ_TN,),
        in_specs=[
            pl.BlockSpec((m, D_MODEL), lambda j: (0, 0)),
            pl.BlockSpec((m, PROJ_TN), lambda j: (0, j)),
            pl.BlockSpec((None, D_MODEL, PROJ_TN), lambda j: (layer, 0, j)),
        ],
        out_specs=pl.BlockSpec((m, PROJ_TN), lambda j: (0, j)),
        out_shape=jax.ShapeDtypeStruct((m, D_MODEL), F32),
        compiler_params=pltpu.CompilerParams(
            dimension_semantics=("parallel",), vmem_limit_bytes=VMEM_LIMIT),
        name="proj_residual_split" if w.dtype == F32 else "proj_residual",
    )(a, x, w)


FFN_TF = 512
FFN_TM = 512
FFN_TF_SAMPLE = 1024


def _ffn_kernel(x_ref, g_ref, wr_ref, br_ref, wg_ref, wu_ref, wd_ref, o_ref, xhi_sc, xlo_sc, comb_sc, *, moe):
    e = pl.program_id(1)
    f = pl.program_id(2)
    precise = wg_ref.dtype == F32

    @pl.when((e == 0) & (f == 0))
    def _():
        x = x_ref[...]
        xn = _rms_norm(x, g_ref[...])
        o_ref[...] = x
        if precise or moe:
            xn_hi, xn_lo = _split(xn)
            xhi_sc[...] = xn_hi
            if precise:
                xlo_sc[...] = xn_lo
        else:
            xhi_sc[...] = xn.astype(BF16)
        if moe:
            logits = _mm(xn_hi, xn_lo, wr_ref[...]) + br_ref[...]
            lane = lax.broadcasted_iota(jnp.int32, logits.shape, 1)
            v1 = jnp.max(logits, axis=-1, keepdims=True)
            i1 = jnp.min(jnp.where(logits == v1, lane, LANES), axis=-1, keepdims=True)
            rest = jnp.where(lane == i1, NEG, logits)
            v2 = jnp.max(rest, axis=-1, keepdims=True)
            i2 = jnp.min(jnp.where(rest == v2, lane, LANES), axis=-1, keepdims=True)
            e2 = jnp.exp(v2 - v1)
            g1 = 1.0 / (1.0 + e2)
            g2 = e2 / (1.0 + e2)
            comb_sc[...] = jnp.where(lane == i1, g1, 0.0) + jnp.where(lane == i2, g2, 0.0)

    xn_hi = xhi_sc[...]
    xn_lo = xlo_sc[...] if precise else None
    gate = _mm(xn_hi, xn_lo, wg_ref[0])
    up = _mm(xn_hi, xn_lo, wu_ref[0])
    h = gate * _sigmoid(gate) * up
    if moe:
        lane = lax.broadcasted_iota(jnp.int32, comb_sc.shape, 1)
        h = h * jnp.sum(jnp.where(lane == e, comb_sc[...], 0.0), axis=-1, keepdims=True)
    h_hi, h_lo = _split(h) if precise else (h.astype(BF16), None)
    o_ref[...] += _mm(h_hi, h_lo, wd_ref[0])


def _ffn(x, g, wr, br, wg, wu, wd, *, layer, moe, tm, tf=FFN_TF):
    m = x.shape[0]
    _, n_e, _, d_f = wg.shape
    n_f = d_f // tf
    lo_rows = tm if wg.dtype == F32 else 16
    return pl.pallas_call(
        functools.partial(_ffn_kernel, moe=moe),
        grid=(m // tm, n_e, n_f),
        in_specs=[
            pl.BlockSpec((tm, D_MODEL), lambda i, e, f: (i, 0)),
            pl.BlockSpec((1, D_MODEL), lambda i, e, f: (0, 0)),
            pl.BlockSpec((None, D_MODEL, LANES), lambda i, e, f: (layer, 0, 0)),
            pl.BlockSpec((None, 1, LANES), lambda i, e, f: (layer, 0, 0)),
            pl.BlockSpec((None, 1, D_MODEL, tf), lambda i, e, f: (layer, e, 0, f)),
            pl.BlockSpec((None, 1, D_MODEL, tf), lambda i, e, f: (layer, e, 0, f)),
            pl.BlockSpec((None, 1, tf, D_MODEL), lambda i, e, f: (layer, e, f, 0)),
        ],
        out_specs=pl.BlockSpec((tm, D_MODEL), lambda i, e, f: (i, 0)),
        out_shape=jax.ShapeDtypeStruct((m, D_MODEL), F32),
        scratch_shapes=[pltpu.VMEM((tm, D_MODEL), BF16), pltpu.VMEM((lo_rows, D_MODEL), BF16),
                        pltpu.VMEM((tm, LANES), F32)],
        compiler_params=pltpu.CompilerParams(
            dimension_semantics=("parallel", "arbitrary", "arbitrary"), vmem_limit_bytes=VMEM_LIMIT),
        name=("ffn_moe" if moe else "ffn_dense") + ("_split" if wg.dtype == F32 else ""),
    )(x, g, wr, br, wg, wu, wd)


MOE_TM = 512
MOE_TF = 1024
MOE_ROUTE_TM = 512
MOE_MOVE_TM = 256


def _route_kernel(x_ref, g_ref, wr_ref, br_ref, o_ref):
    xn_hi, xn_lo = _split(_rms_norm(x_ref[...], g_ref[...]))
    logits = _mm(xn_hi, xn_lo, wr_ref[...]) + br_ref[...]
    lane = lax.broadcasted_iota(jnp.int32, logits.shape, 1)
    v1 = jnp.max(logits, axis=-1, keepdims=True)
    i1 = jnp.min(jnp.where(logits == v1, lane, LANES), axis=-1, keepdims=True)
    rest = jnp.where(lane == i1, NEG, logits)
    v2 = jnp.max(rest, axis=-1, keepdims=True)
    i2 = jnp.min(jnp.where(rest == v2, lane, LANES), axis=-1, keepdims=True)
    e2 = jnp.exp(v2 - v1)
    g1 = 1.0 / (1.0 + e2)
    g2 = e2 / (1.0 + e2)
    o_ref[...] = jnp.where(lane == 0, i1.astype(F32),
                           jnp.where(lane == 1, i2.astype(F32),
                                     jnp.where(lane == 2, g1, jnp.where(lane == 3, g2, 0.0))))


def _route(x, g, wr, br, *, layer):
    m = x.shape[0]
    tm = _row_tile(m, MOE_ROUTE_TM)
    return pl.pallas_call(
        _route_kernel,
        grid=(m // tm,),
        in_specs=[
            pl.BlockSpec((tm, D_MODEL), lambda i: (i, 0)),
            pl.BlockSpec((1, D_MODEL), lambda i: (0, 0)),
            pl.BlockSpec((None, D_MODEL, LANES), lambda i: (layer, 0, 0)),
            pl.BlockSpec((None, 1, LANES), lambda i: (layer, 0, 0)),
        ],
        out_specs=pl.BlockSpec((tm, LANES), lambda i: (i, 0)),
        out_shape=jax.ShapeDtypeStruct((m, LANES), F32),
        compiler_params=pltpu.CompilerParams(dimension_semantics=("parallel",), vmem_limit_bytes=VMEM_LIMIT),
        name="moe_route",
    )(x, g, wr, br)


def _moe_plan(route, n_tiles):
    experts = jnp.arange(N_EXPERTS, dtype=jnp.int32)
    e_pair = route[:, :2].astype(jnp.int32).reshape(-1)
    onehot = (e_pair[:, None] == experts[None, :]).astype(jnp.int32)
    csum = jnp.cumsum(onehot, axis=0)
    rank = jnp.sum(csum * onehot, axis=1) - 1
    tiles_e = (csum[-1] + MOE_TM - 1) // MOE_TM
    tend = jnp.cumsum(tiles_e)
    tstart = tend - tiles_e
    pos = (jnp.sum(onehot * tstart[None, :], axis=1) * MOE_TM + rank).astype(jnp.int32)
    n_used = tend[-1:].astype(jnp.int32)
    t_idx = jnp.arange(n_tiles, dtype=jnp.int32)
    tile_e = jnp.sum((t_idx[:, None] >= tend[None, :]).astype(jnp.int32), axis=1)
    last_e = jnp.max(jnp.where(tiles_e > 0, experts, 0))
    tile_e = jnp.minimum(tile_e, last_e).astype(jnp.int32)
    return pos, tile_e, n_used


def _row_copy(src, dst, sem):
    return pltpu.make_async_copy(src, dst, sem)


def _push_kernel(pos_ref, x_ref, g_ref, zeros_ref, xs_ref, xn_sc, sem):
    del zeros_ref
    tm = x_ref.shape[0]
    base = pl.program_id(0) * tm
    xn_sc[...] = _rms_norm(x_ref[...], g_ref[...])

    def start(r, carry):
        for c in range(2):
            dst = pos_ref[2 * (base + r) + c]
            _row_copy(xn_sc.at[pl.ds(r, 1)], xs_ref.at[pl.ds(dst, 1)], sem).start(priority=c)
        return carry

    lax.fori_loop(0, tm, start, 0, unroll=8)

    def wait(r, carry):
        for c in range(2):
            _row_copy(xn_sc.at[pl.ds(r, 1)], xs_ref.at[pl.ds(0, 1)], sem).wait()
        return carry

    lax.fori_loop(0, tm, wait, 0, unroll=8)


def _push(x, g, pos, n_rows):
    m = x.shape[0]
    tm = _row_tile(m, MOE_MOVE_TM)
    zeros = jnp.zeros((n_rows, D_MODEL), F32)
    return pl.pallas_call(
        _push_kernel,
        grid_spec=pltpu.PrefetchScalarGridSpec(
            num_scalar_prefetch=1,
            grid=(m // tm,),
            in_specs=[
                pl.BlockSpec((tm, D_MODEL), lambda i, pos: (i, 0)),
                pl.BlockSpec((1, D_MODEL), lambda i, pos: (0, 0)),
                pl.BlockSpec(memory_space=pl.ANY),
            ],
            out_specs=pl.BlockSpec(memory_space=pl.ANY),
            scratch_shapes=[pltpu.VMEM((tm, D_MODEL), F32), pltpu.SemaphoreType.DMA],
        ),
        out_shape=jax.ShapeDtypeStruct((n_rows, D_MODEL), F32),
        input_output_aliases={3: 0},
        compiler_params=pltpu.CompilerParams(dimension_semantics=("arbitrary",), vmem_limit_bytes=VMEM_LIMIT),
        name="moe_push",
    )(pos, x, g, zeros)


def _experts_kernel(tile_e_ref, n_used_ref, x_ref, wg_ref, wu_ref, wd_ref, o_ref, xb_sc):
    t = pl.program_id(0)
    f = pl.program_id(1)
    used = t < n_used_ref[0]

    @pl.when(f == 0)
    def _():
        xb_sc[...] = x_ref[...].astype(BF16)
        o_ref[...] = jnp.zeros_like(o_ref)

    @pl.when(used)
    def _():
        xn = xb_sc[...]
        gate = _dot(xn, wg_ref[0])
        up = _dot(xn, wu_ref[0])
        h = gate * _sigmoid(gate) * up
        o_ref[...] += _dot(h.astype(BF16), wd_ref[0])


def _experts(xs, tile_e, n_used, wg, wu, wd, *, layer):
    n_rows = xs.shape[0]
    n_f = wg.shape[3] // MOE_TF
    return pl.pallas_call(
        _experts_kernel,
        grid_spec=pltpu.PrefetchScalarGridSpec(
            num_scalar_prefetch=2,
            grid=(n_rows // MOE_TM, n_f),
            in_specs=[
                pl.BlockSpec((MOE_TM, D_MODEL), lambda t, f, te, nu: (t, 0)),
                pl.BlockSpec((None, 1, D_MODEL, MOE_TF), lambda t, f, te, nu: (layer, te[t], 0, f)),
                pl.BlockSpec((None, 1, D_MODEL, MOE_TF), lambda t, f, te, nu: (layer, te[t], 0, f)),
                pl.BlockSpec((None, 1, MOE_TF, D_MODEL), lambda t, f, te, nu: (layer, te[t], f, 0)),
            ],
            out_specs=pl.BlockSpec((MOE_TM, D_MODEL), lambda t, f, te, nu: (t, 0)),
            scratch_shapes=[pltpu.VMEM((MOE_TM, D_MODEL), BF16)],
        ),
        out_shape=jax.ShapeDtypeStruct((n_rows, D_MODEL), F32),
        compiler_params=pltpu.CompilerParams(
            dimension_semantics=("arbitrary", "arbitrary"), vmem_limit_bytes=VMEM_LIMIT),
        name="moe_experts",
    )(tile_e, n_used, xs, wg, wu, wd)


def _combine_kernel(pos_ref, x_ref, route_ref, ys_ref, o_ref, y_sc, sems):
    tm = x_ref.shape[0]
    i = pl.program_id(0)

    def gather(tile, slot):
        base = tile * tm

        def start(r, carry):
            for c in range(2):
                src = pos_ref[2 * (base + r) + c]
                _row_copy(ys_ref.at[pl.ds(src, 1)], y_sc.at[slot, c, pl.ds(r, 1)], sems.at[slot]).start(priority=c)
            return carry

        lax.fori_loop(0, tm, start, 0, unroll=8)

    @pl.when(i == 0)
    def _():
        gather(0, 0)

    @pl.when(i + 1 < pl.num_programs(0))
    def _():
        gather(i + 1, (i + 1) % 2)

    slot = i % 2

    def wait(r, carry):
        for c in range(2):
            _row_copy(ys_ref.at[pl.ds(0, 1)], y_sc.at[slot, c, pl.ds(r, 1)], sems.at[slot]).wait()
        return carry

    lax.fori_loop(0, tm, wait, 0, unroll=8)
    o_ref[...] = x_ref[...] + route_ref[:, 2:3] * y_sc[slot, 0] + route_ref[:, 3:4] * y_sc[slot, 1]


def _combine(x, route, pos, ys):
    m = x.shape[0]
    tm = _row_tile(m, MOE_MOVE_TM)
    return pl.pallas_call(
        _combine_kernel,
        grid_spec=pltpu.PrefetchScalarGridSpec(
            num_scalar_prefetch=1,
            grid=(m // tm,),
            in_specs=[
                pl.BlockSpec((tm, D_MODEL), lambda i, pos: (i, 0)),
                pl.BlockSpec((tm, LANES), lambda i, pos: (i, 0)),
                pl.BlockSpec(memory_space=pl.ANY),
            ],
            out_specs=pl.BlockSpec((tm, D_MODEL), lambda i, pos: (i, 0)),
            scratch_shapes=[pltpu.VMEM((2, 2, tm, D_MODEL), F32), pltpu.SemaphoreType.DMA((2,))],
        ),
        out_shape=jax.ShapeDtypeStruct((m, D_MODEL), F32),
        compiler_params=pltpu.CompilerParams(dimension_semantics=("arbitrary",), vmem_limit_bytes=VMEM_LIMIT),
        name="moe_combine",
    )(pos, x, route, ys)


def _moe_routed(x, g, wr, br, wg, wu, wd, *, layer):
    m = x.shape[0]
    n_tiles = (2 * m) // MOE_TM + N_EXPERTS
    route = _route(x, g, wr, br, layer=layer)
    pos, tile_e, n_used = _moe_plan(route, n_tiles)
    xs = _push(x, g, pos, n_tiles * MOE_TM)
    ys = _experts(xs, tile_e, n_used, wg, wu, wd, layer=layer)
    return _combine(x, route, pos, ys)


def _final_norm_kernel(x_ref, g_ref, o_ref):
    o_ref[...] = _rms_norm(x_ref[...], g_ref[...])


def _final_norm(x, g, *, tm):
    m = x.shape[0]
    return pl.pallas_call(
        _final_norm_kernel,
        grid=(m // tm,),
        in_specs=[pl.BlockSpec((tm, D_MODEL), lambda i: (i, 0)), pl.BlockSpec((1, D_MODEL), lambda i: (0, 0))],
        out_specs=pl.BlockSpec((tm, D_MODEL), lambda i: (i, 0)),
        out_shape=jax.ShapeDtypeStruct((m, D_MODEL), F32),
        compiler_params=pltpu.CompilerParams(dimension_semantics=("parallel",), vmem_limit_bytes=VMEM_LIMIT),
        name="final_norm",
    )(x, g)


def _rope_tables(pos):
    half = HEAD_DIM // 2
    inv = ROPE_THETA ** (-2.0 * jnp.arange(half, dtype=F32) / HEAD_DIM)
    ang = pos.astype(F32)[:, None] * inv[None, :]
    cos, sin = jnp.cos(ang), jnp.sin(ang)
    reps = LANES // HEAD_DIM
    return (jnp.tile(jnp.concatenate([cos, cos], axis=-1), (1, reps)),
            jnp.tile(jnp.concatenate([-sin, sin], axis=-1), (1, reps)))


def _row_tile(m, cap):
    t = cap
    while m % t:
        t //= 2
    return t


def _put_tail(main, tail, batch, seq):
    c = main.shape[1]
    upd = tail.reshape(batch, BLOCK, c).astype(main.dtype)
    return lax.dynamic_update_slice(main.reshape(batch, seq, c), upd, (0, seq - BLOCK, 0)).reshape(batch * seq, c)


def kernel(x_prompt, x_sample, cache_k, cache_v, state_pool, norm_mix, norm_ffn, norm_final, w_in, attn_sinks,
           w_pool_map, pool_scale, w_out, w_ffn_gate, w_ffn_up, w_ffn_down, w_router, b_router, w_exp_gate,
           w_exp_up, w_exp_down):
    batch, seq, _ = x_prompt.shape
    dec_batch, dec_seq, _ = x_sample.shape
    depth = w_in.shape[0]
    assert seq % BLOCK == 0 and dec_batch % SAMPLE_DB == 0 and dec_seq <= POOL_HIST

    w_in_b = w_in.astype(BF16)
    w_pool_b = w_pool_map.astype(BF16)
    w_out_b = w_out.astype(BF16)
    w_fg, w_fu, w_fd = (w[:, None] for w in (w_ffn_gate, w_ffn_up, w_ffn_down))
    w_fg_b, w_fu_b, w_fd_b = (w.astype(BF16) for w in (w_fg, w_fu, w_fd))
    w_eg_b, w_eu_b, w_ed_b = (w.astype(BF16) for w in (w_exp_gate, w_exp_up, w_exp_down))
    w_r = jnp.pad(w_router.astype(F32), ((0, 0), (0, 0), (0, LANES - N_EXPERTS)))
    b_r = jnp.pad(b_router.astype(F32), ((0, 0), (0, LANES - N_EXPERTS)), constant_values=NEG)[:, None]
    zero_wr = jnp.zeros((w_fg.shape[0], D_MODEL, LANES), F32)
    zero_br = jnp.zeros((w_fg.shape[0], 1, LANES), F32)

    pos_p = jnp.arange(seq, dtype=jnp.int32)
    pos_s = jnp.tile(PAST_LEN + jnp.arange(dec_seq, dtype=jnp.int32), dec_batch)
    cos_p, sin_p = _rope_tables(pos_p)
    cos_s, sin_s = _rope_tables(pos_s)
    cos_x, sin_x = _rope_tables(jnp.concatenate([jnp.tile(pos_p[seq - BLOCK:], batch), pos_s]))

    m_p, m_s, m_t = batch * seq, dec_batch * dec_seq, batch * BLOCK
    tm_in_p = _row_tile(seq, IN_TM)
    tm_ffn_p = _row_tile(m_p, FFN_TM)

    xp = x_prompt.reshape(m_p, D_MODEL)
    xs = x_sample.reshape(m_s, D_MODEL)
    ck = cache_k.reshape(depth, dec_batch, WINDOW, KV_DIM)
    cv = cache_v.reshape(depth, dec_batch, WINDOW, KV_DIM)

    n_split = min(2, depth)
    xx = jnp.concatenate([x_prompt[:, seq - BLOCK:].reshape(m_t, D_MODEL), xs], axis=0)

    kp_l, vp_l, pp_l, ks_l, vs_l, ps_l = [], [], [], [], [], []
    for l in range(depth):
        g_mix = norm_mix[l][None]
        g_ffn = norm_ffn[l][None]
        pscale = pool_scale[l][None]
        sinks = attn_sinks[l]
        split = l < n_split

        q, k, v, u, ga, gp = _inproj(xp, g_mix, w_in_b, cos_p, sin_p, layer=l, tm=tm_in_p)
        if split:
            qx, kx, vx, ux, gax, gpx = _inproj(xx, g_mix, w_in, cos_x, sin_x, layer=l, tm=m_t + m_s)
            k, v, u = (_put_tail(a, b[:m_t], batch, seq) for a, b in ((k, kx), (v, vx), (u, ux)))
        xp = _mixer_prompt(q, k, v, u, ga, gp, xp, sinks, w_pool_b, pscale, w_out_b, layer=l, batch=batch, seq=seq)
        if split:
            mix_t = _mixer_prompt_tail(qx[:m_t], k, v, u, gax[:m_t], gpx[:m_t], sinks, w_pool_map, pscale,
                                       layer=l, batch=batch, seq=seq)
            mix_s, nk, nv, npool = _mixer_sample(qx[m_t:], kx[m_t:], vx[m_t:], ux[m_t:], gax[m_t:], gpx[m_t:],
                                                 ck, cv, state_pool, sinks, w_pool_map, pscale,
                                                 layer=l, dec_seq=dec_seq)
            xx = _proj_residual(jnp.concatenate([mix_t, mix_s], axis=0), xx, w_out, layer=l)
            xp = _put_tail(xp, xx[:m_t], batch, seq)
        else:
            qs, ks, vs, us, gas, gps = _inproj(xs, g_mix, w_in_b, cos_s, sin_s, layer=l, tm=m_s)
            mix_s, nk, nv, npool = _mixer_sample(qs, ks, vs, us, gas, gps, ck, cv, state_pool, sinks,
                                                 w_pool_b, pscale, layer=l, dec_seq=dec_seq)
            xs = _proj_residual(mix_s, xs, w_out_b, layer=l)
        kp_l.append(k.reshape(batch, seq, N_KV_HEADS, HEAD_DIM)[:, seq - WINDOW:])
        vp_l.append(v.reshape(batch, seq, N_KV_HEADS, HEAD_DIM)[:, seq - WINDOW:])
        pp_l.append(u.reshape(batch, seq, POOL_DIM)[:, seq - POOL_HIST:])
        ks_l.append(nk.reshape(dec_batch, WINDOW, N_KV_HEADS, HEAD_DIM))
        vs_l.append(nv.reshape(dec_batch, WINDOW, N_KV_HEADS, HEAD_DIM))
        ps_l.append(npool)

        i = l // 2
        if l % 2 == 0:
            xp = _ffn(xp, g_ffn, zero_wr, zero_br, w_fg_b, w_fu_b, w_fd_b, layer=i, moe=False, tm=tm_ffn_p)
            if split and l + 1 < n_split:
                xx = _ffn(xx, g_ffn, zero_wr, zero_br, w_fg, w_fu, w_fd, layer=i, moe=False, tm=m_t + m_s)
                xp = _put_tail(xp, xx[:m_t], batch, seq)
            else:
                if split:
                    xs = xx[m_t:]
                xs = _ffn(xs, g_ffn, zero_wr, zero_br, w_fg_b, w_fu_b, w_fd_b, layer=i, moe=False, tm=m_s)
        else:
            if split:
                xs = xx[m_t:]
            xp = _moe_routed(xp, g_ffn, w_r, b_r, w_eg_b, w_eu_b, w_ed_b, layer=i)
            xs = _ffn(xs, g_ffn, w_r, b_r, w_eg_b, w_eu_b, w_ed_b, layer=i, moe=True, tm=m_s, tf=FFN_TF_SAMPLE)

    g_fin = norm_final[None]
    y_p = _final_norm(xp, g_fin, tm=tm_ffn_p).reshape(batch, seq, D_MODEL)
    y_s = _final_norm(xs, g_fin, tm=m_s).reshape(dec_batch, dec_seq, D_MODEL)
    return (y_p, y_s, jnp.stack(kp_l), jnp.stack(vp_l), jnp.stack(pp_l),
            jnp.stack(ks_l), jnp.stack(vs_l), jnp.stack(ps_l))
```

```python
import functools

import jax
import jax.numpy as jnp
from jax import lax
from jax.experimental import pallas as pl
from jax.experimental.pallas import tpu as pltpu

D_MODEL = 2048
HEAD_DIM = 64
N_HEADS = 32
N_KV_HEADS = 4
GROUP = N_HEADS // N_KV_HEADS
KV_DIM = N_KV_HEADS * HEAD_DIM
WINDOW = 128
BLOCK = 128
ROPE_THETA = 10000.0
PAST_LEN = 16384
POOL_WINDOWS = (2, 4, 8, 16)
POOL_GROUP_DIM = 256
POOL_DIM = 1024
POOL_OUT_GROUP = 512
POOL_HIST = 15
N_EXPERTS = 8
EPS = 1e-5
NEG = -1e30

BF16 = jnp.bfloat16
F32 = jnp.float32

LANES = 128
VMEM_LIMIT = 52 * 1024 * 1024

IN_TN = 512
IN_TM = 1024
IN_Q_TILES = D_MODEL // IN_TN
IN_KV_TILE = IN_Q_TILES
IN_U_START = IN_KV_TILE + 1
IN_U_TILES = POOL_DIM // IN_TN
IN_GA_START = IN_U_START + IN_U_TILES
IN_G_TILES = D_MODEL // IN_TN
IN_GP_START = IN_GA_START + IN_G_TILES
IN_TILES = IN_GP_START + IN_G_TILES


def _sigmoid(x):
    return 1.0 / (1.0 + jnp.exp(-x))


def _split(a):
    hi = a.astype(BF16)
    lo = (a - hi.astype(F32)).astype(BF16)
    return hi, lo


def _dot(a, b):
    return jnp.dot(a, b, preferred_element_type=F32)


def _dot_nt(a, b):
    return lax.dot_general(a, b, (((1,), (1,)), ((), ())), preferred_element_type=F32)


def _mm(a_hi, a_lo, w):
    if w.dtype == BF16:
        return _dot(a_hi, w)
    w_hi, w_lo = _split(w)
    return _dot(a_hi, w_hi) + (_dot(a_lo, w_hi) + _dot(a_hi, w_lo))


def _qk(q, k_hi, k_lo):
    if q.dtype == BF16:
        return _dot_nt(q, k_hi)
    q_hi, q_lo = _split(q)
    return _dot_nt(q_hi, k_hi) + (_dot_nt(q_lo, k_hi) + _dot_nt(q_hi, k_lo))


def _pv(p, v_hi, v_lo):
    if v_lo is None:
        return _dot(p.astype(BF16), v_hi)
    p_hi, p_lo = _split(p)
    return _dot(p_hi, v_hi) + (_dot(p_lo, v_hi) + _dot(p_hi, v_lo))


def _kv_operands(t, precise):
    return _split(t) if precise else (t.astype(BF16), None)


def _cols(t, c):
    return None if t is None else t[:, c]


def _pool_map(pm, w_ref, gi):
    pm_hi, pm_lo = _split(pm) if w_ref.dtype == F32 else (pm.astype(BF16), None)
    return _mm(pm_hi, pm_lo, w_ref[gi])


def _rms_norm(x, g):
    ms = jnp.mean(x * x, axis=-1, keepdims=True)
    return x * lax.rsqrt(ms + EPS) * g


def _rope(t, cos, sin_signed):
    width = t.shape[1]
    lane = lax.broadcasted_iota(jnp.int32, t.shape, 1)
    first_half = (lane % HEAD_DIM) < (HEAD_DIM // 2)
    rot = jnp.where(first_half, pltpu.roll(t, width - HEAD_DIM // 2, 1), pltpu.roll(t, HEAD_DIM // 2, 1))
    reps = width // cos.shape[1]
    return t * jnp.tile(cos, (1, reps)) + rot * jnp.tile(sin_signed, (1, reps))


def _inproj_kernel(x_ref, g_ref, w_ref, cos_ref, sin_ref, q_ref, k_ref, v_ref, u_ref, ga_ref, gp_ref,
                   xhi_sc, xlo_sc):
    j = pl.program_id(1)
    precise = w_ref.dtype == F32

    @pl.when(j == 0)
    def _():
        xn = _rms_norm(x_ref[...], g_ref[...])
        if precise:
            xhi_sc[...], xlo_sc[...] = _split(xn)
        else:
            xhi_sc[...] = xn.astype(BF16)

    def z():
        return _mm(xhi_sc[...], xlo_sc[...] if precise else None, w_ref[...])

    @pl.when(j < IN_Q_TILES)
    def _():
        q_ref[...] = (_rope(z(), cos_ref[...], sin_ref[...]) * (HEAD_DIM ** -0.5)).astype(q_ref.dtype)

    @pl.when(j == IN_KV_TILE)
    def _():
        kv = z()
        k_ref[...] = _rope(kv[:, :KV_DIM], cos_ref[...], sin_ref[...])
        v_ref[...] = kv[:, KV_DIM:]

    @pl.when((j >= IN_U_START) & (j < IN_GA_START))
    def _():
        u_ref[...] = z()

    @pl.when((j >= IN_GA_START) & (j < IN_GP_START))
    def _():
        ga_ref[...] = z()

    @pl.when(j >= IN_GP_START)
    def _():
        gp_ref[...] = z()


def _inproj(x, g, w, cos, sin, *, layer, tm):
    m = x.shape[0]
    pos_blocks = cos.shape[0] // tm
    grid = (m // tm, IN_TILES)

    def clip(j, lo, n):
        return jnp.clip(j - lo, 0, n - 1)

    lo_rows = tm if w.dtype == F32 else 16
    return pl.pallas_call(
        _inproj_kernel,
        grid=grid,
        in_specs=[
            pl.BlockSpec((tm, D_MODEL), lambda i, j: (i, 0), pipeline_mode=pl.Buffered(1)),
            pl.BlockSpec((1, D_MODEL), lambda i, j: (0, 0)),
            pl.BlockSpec((None, D_MODEL, IN_TN), lambda i, j: (layer, 0, j)),
            pl.BlockSpec((tm, LANES), lambda i, j: (i % pos_blocks, 0)),
            pl.BlockSpec((tm, LANES), lambda i, j: (i % pos_blocks, 0)),
        ],
        out_specs=[
            pl.BlockSpec((tm, IN_TN), lambda i, j: (i, clip(j, 0, IN_Q_TILES))),
            pl.BlockSpec((tm, KV_DIM), lambda i, j: (i, 0)),
            pl.BlockSpec((tm, KV_DIM), lambda i, j: (i, 0)),
            pl.BlockSpec((tm, IN_TN), lambda i, j: (i, clip(j, IN_U_START, IN_U_TILES))),
            pl.BlockSpec((tm, IN_TN), lambda i, j: (i, clip(j, IN_GA_START, IN_G_TILES))),
            pl.BlockSpec((tm, IN_TN), lambda i, j: (i, clip(j, IN_GP_START, IN_G_TILES))),
        ],
        out_shape=[
            jax.ShapeDtypeStruct((m, D_MODEL), w.dtype),
            jax.ShapeDtypeStruct((m, KV_DIM), F32),
            jax.ShapeDtypeStruct((m, KV_DIM), F32),
            jax.ShapeDtypeStruct((m, POOL_DIM), F32),
            jax.ShapeDtypeStruct((m, D_MODEL), F32),
            jax.ShapeDtypeStruct((m, D_MODEL), F32),
        ],
        scratch_shapes=[pltpu.VMEM((tm, D_MODEL), BF16), pltpu.VMEM((lo_rows, D_MODEL), BF16)],
        compiler_params=pltpu.CompilerParams(
            dimension_semantics=("parallel", "arbitrary"), vmem_limit_bytes=VMEM_LIMIT),
        name="inproj_split" if w.dtype == F32 else "inproj",
    )(x, g, w, cos, sin)


def _mixer_prompt_kernel(sinks_ref, q_ref, kc_ref, kp_ref, vc_ref, vp_ref, uc_ref, up_ref, ga_ref, gp_ref,
                         wpool_ref, pscale_ref, *rest, first_block, n_blocks):
    precise = q_ref.dtype == F32
    if precise:
        mix_ref, ext_sc = rest
        n = first_block + pl.program_id(1)
    else:
        x_ref, *wout_refs, o_ref, ext_sc, mix_ref, prev_sc = rest
        step = pl.program_id(1)
        n = jnp.minimum(step, n_blocks - 1)

        @pl.when((pl.program_id(0) == 0) & (step == 0))
        def _():
            mix_ref[...] = jnp.zeros_like(mix_ref)

        prev_sc[...] = mix_ref[...]
    not_first = n > 0

    kb = jnp.concatenate([kp_ref[...], kc_ref[...]], axis=0)
    vb = jnp.concatenate([vp_ref[...], vc_ref[...]], axis=0)
    qi = lax.broadcasted_iota(jnp.int32, (BLOCK, 2 * BLOCK), 0)
    kj = lax.broadcasted_iota(jnp.int32, (BLOCK, 2 * BLOCK), 1)
    dist = qi + BLOCK - kj
    mask = (dist >= 0) & (dist <= WINDOW) & ((kj >= BLOCK) | not_first)
    lane = lax.broadcasted_iota(jnp.int32, (BLOCK, 2 * HEAD_DIM), 1)
    zeros_kv = jnp.zeros((2 * BLOCK, HEAD_DIM), F32)
    sum_cols = [(lax.broadcasted_iota(jnp.int32, (2 * BLOCK, 2 * HEAD_DIM), 1) == c).astype(F32) for c in range(2)]

    ext_sc[0:16, :] = jnp.where(not_first, up_ref[BLOCK - 16:BLOCK, :], 0.0)
    ext_sc[16:16 + BLOCK, :] = uc_ref[...]
    pos = n * BLOCK + lax.broadcasted_iota(jnp.int32, (BLOCK, 1), 0)

    for kh in range(N_KV_HEADS):
        w = POOL_WINDOWS[kh]
        pcs = slice(kh * POOL_GROUP_DIM, (kh + 1) * POOL_GROUP_DIM)
        cur = ext_sc[16:16 + BLOCK, pcs]
        acc = cur
        for d in range(1, w):
            acc = acc + ext_sc[16 - d:16 - d + BLOCK, pcs]
        inv_cnt = 1.0 / jnp.minimum(pos + 1, w).astype(F32)
        pm = acc * inv_cnt - cur
        ocs = slice(kh * POOL_OUT_GROUP, (kh + 1) * POOL_OUT_GROUP)
        pg = _pool_map(pm, wpool_ref, kh) * pscale_ref[:, ocs]

        hs = slice(kh * HEAD_DIM, (kh + 1) * HEAD_DIM)
        k_h, v_h = kb[:, hs], vb[:, hs]
        k_op = _kv_operands(jnp.concatenate([jnp.concatenate([k_h, zeros_kv], axis=1),
                                             jnp.concatenate([zeros_kv, k_h], axis=1)], axis=0), precise)
        v_op = _kv_operands(jnp.concatenate([jnp.concatenate([v_h, zeros_kv, sum_cols[0]], axis=1),
                                             jnp.concatenate([zeros_kv, v_h, sum_cols[1]], axis=1)], axis=0), precise)
        pairs = []
        for pair in range(GROUP // 2):
            slab = (kh * GROUP // 2 + pair) * 2 * HEAD_DIM
            s2 = _qk(q_ref[:, slab:slab + 2 * HEAD_DIM], *k_op)
            probs, sink_terms = [], []
            for c in range(2):
                s = jnp.where(mask, s2[:, c * 2 * BLOCK:(c + 1) * 2 * BLOCK], NEG)
                sink = sinks_ref[kh * GROUP + 2 * pair + c]
                m = jnp.maximum(jnp.max(s, axis=-1, keepdims=True), sink)
                probs.append(jnp.exp(s - m))
                sink_terms.append(jnp.exp(sink - m))
            out = _pv(jnp.concatenate(probs, axis=1), *v_op)
            inv = [1.0 / (out[:, 2 * HEAD_DIM + c:2 * HEAD_DIM + c + 1] + sink_terms[c]) for c in range(2)]
            pairs.append(out[:, :2 * HEAD_DIM] * jnp.where(lane < HEAD_DIM, inv[0], inv[1]))
        a = jnp.concatenate(pairs, axis=1)
        mix = _sigmoid(ga_ref[:, ocs]) * a + _sigmoid(gp_ref[:, ocs]) * pg
        mix_ref[:, ocs] = mix.astype(mix_ref.dtype)
        if not precise:
            o_ref[:, ocs] = x_ref[:, ocs] + _dot(prev_sc[...], wout_refs[kh][...])


def _mixer_prompt(q, k, v, u, ga, gp, x, sinks, wpool, pscale, wout, *, layer, batch, seq):
    nb = seq // BLOCK
    m = batch * seq

    def cur(b, s):
        return (b * nb + jnp.minimum(s, nb - 1), 0)

    def prev(b, s):
        return (jnp.maximum(b * nb + jnp.minimum(s, nb - 1) - 1, 0), 0)

    def proj(b, s):
        return (b * nb + jnp.maximum(s - 1, 0), 0)

    return pl.pallas_call(
        functools.partial(_mixer_prompt_kernel, first_block=0, n_blocks=nb),
        grid=(batch, nb + 1),
        in_specs=[
            pl.BlockSpec(memory_space=pltpu.SMEM),
            pl.BlockSpec((BLOCK, D_MODEL), cur),
            pl.BlockSpec((BLOCK, KV_DIM), cur),
            pl.BlockSpec((BLOCK, KV_DIM), prev),
            pl.BlockSpec((BLOCK, KV_DIM), cur),
            pl.BlockSpec((BLOCK, KV_DIM), prev),
            pl.BlockSpec((BLOCK, POOL_DIM), cur),
            pl.BlockSpec((BLOCK, POOL_DIM), prev),
            pl.BlockSpec((BLOCK, D_MODEL), cur),
            pl.BlockSpec((BLOCK, D_MODEL), cur),
            pl.BlockSpec((None, N_KV_HEADS, POOL_GROUP_DIM, POOL_OUT_GROUP), lambda b, n: (layer, 0, 0, 0)),
            pl.BlockSpec((1, D_MODEL), lambda b, n: (0, 0)),
            pl.BlockSpec((BLOCK, D_MODEL), proj),
            *[pl.BlockSpec((None, D_MODEL, POOL_OUT_GROUP), functools.partial(lambda b, n, c: (layer, 0, c), c=c))
              for c in range(N_KV_HEADS)],
        ],
        out_specs=pl.BlockSpec((BLOCK, D_MODEL), proj),
        out_shape=jax.ShapeDtypeStruct((m, D_MODEL), F32),
        scratch_shapes=[pltpu.VMEM((16 + BLOCK, POOL_DIM), F32), pltpu.VMEM((BLOCK, D_MODEL), BF16),
                        pltpu.VMEM((BLOCK, D_MODEL), BF16)],
        compiler_params=pltpu.CompilerParams(
            dimension_semantics=("arbitrary", "arbitrary"), vmem_limit_bytes=VMEM_LIMIT),
        name="mixer_prompt",
    )(sinks, q, k, k, v, v, u, u, ga, gp, wpool, pscale, x, *([wout] * N_KV_HEADS))


def _mixer_prompt_tail(q_t, k, v, u, ga_t, gp_t, sinks, wpool, pscale, *, layer, batch, seq):
    nb = seq // BLOCK

    def tail(b, n):
        return (b, 0)

    def cur(b, n):
        return (b * nb + nb - 1, 0)

    def prev(b, n):
        return (jnp.maximum(b * nb + nb - 2, 0), 0)

    return pl.pallas_call(
        functools.partial(_mixer_prompt_kernel, first_block=nb - 1, n_blocks=nb),
        grid=(batch, 1),
        in_specs=[
            pl.BlockSpec(memory_space=pltpu.SMEM),
            pl.BlockSpec((BLOCK, D_MODEL), tail),
            pl.BlockSpec((BLOCK, KV_DIM), cur),
            pl.BlockSpec((BLOCK, KV_DIM), prev),
            pl.BlockSpec((BLOCK, KV_DIM), cur),
            pl.BlockSpec((BLOCK, KV_DIM), prev),
            pl.BlockSpec((BLOCK, POOL_DIM), cur),
            pl.BlockSpec((BLOCK, POOL_DIM), prev),
            pl.BlockSpec((BLOCK, D_MODEL), tail),
            pl.BlockSpec((BLOCK, D_MODEL), tail),
            pl.BlockSpec((None, N_KV_HEADS, POOL_GROUP_DIM, POOL_OUT_GROUP), lambda b, n: (layer, 0, 0, 0)),
            pl.BlockSpec((1, D_MODEL), lambda b, n: (0, 0)),
        ],
        out_specs=pl.BlockSpec((BLOCK, D_MODEL), tail),
        out_shape=jax.ShapeDtypeStruct((batch * BLOCK, D_MODEL), F32),
        scratch_shapes=[pltpu.VMEM((16 + BLOCK, POOL_DIM), F32)],
        compiler_params=pltpu.CompilerParams(
            dimension_semantics=("parallel", "arbitrary"), vmem_limit_bytes=VMEM_LIMIT),
        name="mixer_prompt_tail",
    )(sinks, q_t, k, k, v, v, u, u, ga_t, gp_t, wpool, pscale)


SAMPLE_DB = 8


def _mixer_sample_kernel(sinks_ref, q_ref, kn_ref, vn_ref, u_ref, ga_ref, gp_ref, ck_ref, cv_ref, pool_ref,
                         wpool_ref, pscale_ref, mix_ref, nk_ref, nv_ref, npool_ref, ext_sc, pm_sc, a_sc,
                         *, dec_seq):
    t_new = dec_seq
    rows = SAMPLE_DB * t_new
    precise = q_ref.dtype == F32
    kc_hi, kc_lo = _kv_operands(ck_ref[...].reshape(SAMPLE_DB * WINDOW, KV_DIM), precise)
    vc_hi, vc_lo = _kv_operands(cv_ref[...].reshape(SAMPLE_DB * WINDOW, KV_DIM), precise)
    kn_hi, kn_lo = _kv_operands(kn_ref[...], precise)
    vn_hi, vn_lo = _kv_operands(vn_ref[...], precise)

    rho = lax.broadcasted_iota(jnp.int32, (GROUP * rows, 1), 0)
    q_db = (rho % rows) // t_new
    q_t = rho % t_new
    q_g = rho // rows
    colc = lax.broadcasted_iota(jnp.int32, (1, SAMPLE_DB * WINDOW), 1)
    mask_c = (q_db == colc // WINDOW) & ((colc % WINDOW) >= q_t)
    coln = lax.broadcasted_iota(jnp.int32, (1, rows), 1)
    mask_n = (q_db == coln // t_new) & ((coln % t_new) <= q_t)

    for d in range(SAMPLE_DB):
        rs = slice(d * t_new, (d + 1) * t_new)
        ext_sc[d, 0:POOL_HIST, :] = pool_ref[d]
        ext_sc[d, POOL_HIST:POOL_HIST + t_new, :] = u_ref[rs, :]
        npool_ref[d] = ext_sc[d, t_new:t_new + POOL_HIST, :]
        nk_ref[d, 0:WINDOW - t_new, :] = ck_ref[d, t_new:WINDOW, :]
        nk_ref[d, WINDOW - t_new:WINDOW, :] = kn_ref[rs, :]
        nv_ref[d, 0:WINDOW - t_new, :] = cv_ref[d, t_new:WINDOW, :]
        nv_ref[d, WINDOW - t_new:WINDOW, :] = vn_ref[rs, :]
        for gi, w in enumerate(POOL_WINDOWS):
            pcs = slice(gi * POOL_GROUP_DIM, (gi + 1) * POOL_GROUP_DIM)
            cur = ext_sc[d, POOL_HIST:POOL_HIST + t_new, pcs]
            acc = cur
            for j in range(1, w):
                acc = acc + ext_sc[d, POOL_HIST - j:POOL_HIST - j + t_new, pcs]
            pm_sc[rs, pcs] = acc / float(w) - cur

    for kh in range(N_KV_HEADS):
        hs = slice(kh * HEAD_DIM, (kh + 1) * HEAD_DIM)
        qs = jnp.concatenate(
            [q_ref[:, (kh * GROUP + g) * HEAD_DIM:(kh * GROUP + g + 1) * HEAD_DIM] for g in range(GROUP)], axis=0)
        sc = jnp.where(mask_c, _qk(qs, kc_hi[:, hs], _cols(kc_lo, hs)), NEG)
        sn = jnp.where(mask_n, _qk(qs, kn_hi[:, hs], _cols(kn_lo, hs)), NEG)
        sink = jnp.zeros((GROUP * rows, 1), F32)
        for g in range(GROUP):
            sink = jnp.where(q_g == g, sinks_ref[kh * GROUP + g], sink)
        m = jnp.maximum(jnp.maximum(jnp.max(sc, axis=-1, keepdims=True), jnp.max(sn, axis=-1, keepdims=True)), sink)
        pc = jnp.exp(sc - m)
        pn = jnp.exp(sn - m)
        denom = jnp.sum(pc, axis=-1, keepdims=True) + jnp.sum(pn, axis=-1, keepdims=True) + jnp.exp(sink - m)
        o = (_pv(pc, vc_hi[:, hs], _cols(vc_lo, hs)) + _pv(pn, vn_hi[:, hs], _cols(vn_lo, hs))) / denom
        for g in range(GROUP):
            h = kh * GROUP + g
            a_sc[:, h * HEAD_DIM:(h + 1) * HEAD_DIM] = o[g * rows:(g + 1) * rows, :]

    for gi in range(N_KV_HEADS):
        pcs = slice(gi * POOL_GROUP_DIM, (gi + 1) * POOL_GROUP_DIM)
        ocs = slice(gi * POOL_OUT_GROUP, (gi + 1) * POOL_OUT_GROUP)
        pg = _pool_map(pm_sc[:, pcs], wpool_ref, gi) * pscale_ref[:, ocs]
        mix_ref[:, ocs] = _sigmoid(ga_ref[:, ocs]) * a_sc[:, ocs] + _sigmoid(gp_ref[:, ocs]) * pg


def _mixer_sample(q, kn, vn, u, ga, gp, ck, cv, pool, sinks, wpool, pscale, *, layer, dec_seq):
    db = ck.shape[1]
    rows = SAMPLE_DB * dec_seq
    m = db * dec_seq
    row_blk = lambda i: (i, 0)
    db_blk = lambda i: (i, 0, 0)
    db_layer_blk = lambda i: (layer, i, 0, 0)
    return pl.pallas_call(
        functools.partial(_mixer_sample_kernel, dec_seq=dec_seq),
        grid=(db // SAMPLE_DB,),
        in_specs=[
            pl.BlockSpec(memory_space=pltpu.SMEM),
            pl.BlockSpec((rows, D_MODEL), row_blk),
            pl.BlockSpec((rows, KV_DIM), row_blk),
            pl.BlockSpec((rows, KV_DIM), row_blk),
            pl.BlockSpec((rows, POOL_DIM), row_blk),
            pl.BlockSpec((rows, D_MODEL), row_blk),
            pl.BlockSpec((rows, D_MODEL), row_blk),
            pl.BlockSpec((None, SAMPLE_DB, WINDOW, KV_DIM), db_layer_blk),
            pl.BlockSpec((None, SAMPLE_DB, WINDOW, KV_DIM), db_layer_blk),
            pl.BlockSpec((None, SAMPLE_DB, POOL_HIST, POOL_DIM), db_layer_blk),
            pl.BlockSpec((None, N_KV_HEADS, POOL_GROUP_DIM, POOL_OUT_GROUP), lambda i: (layer, 0, 0, 0)),
            pl.BlockSpec((1, D_MODEL), lambda i: (0, 0)),
        ],
        out_specs=[
            pl.BlockSpec((rows, D_MODEL), row_blk),
            pl.BlockSpec((SAMPLE_DB, WINDOW, KV_DIM), db_blk),
            pl.BlockSpec((SAMPLE_DB, WINDOW, KV_DIM), db_blk),
            pl.BlockSpec((SAMPLE_DB, POOL_HIST, POOL_DIM), db_blk),
        ],
        out_shape=[
            jax.ShapeDtypeStruct((m, D_MODEL), F32),
            jax.ShapeDtypeStruct((db, WINDOW, KV_DIM), F32),
            jax.ShapeDtypeStruct((db, WINDOW, KV_DIM), F32),
            jax.ShapeDtypeStruct((db, POOL_HIST, POOL_DIM), F32),
        ],
        scratch_shapes=[
            pltpu.VMEM((SAMPLE_DB, 24, POOL_DIM), F32),
            pltpu.VMEM((rows, POOL_DIM), F32),
            pltpu.VMEM((rows, D_MODEL), F32),
        ],
        compiler_params=pltpu.CompilerParams(
            dimension_semantics=("arbitrary",), vmem_limit_bytes=VMEM_LIMIT),
        name="mixer_sample_split" if q.dtype == F32 else "mixer_sample",
    )(sinks, q, kn, vn, u, ga, gp, ck, cv, pool, wpool, pscale)


PROJ_TN = 512


def _proj_residual_kernel(a_ref, x_ref, w_ref, o_ref):
    a = a_ref[...]
    a_hi, a_lo = _split(a) if w_ref.dtype == F32 else (a.astype(BF16), None)
    o_ref[...] = x_ref[...] + _mm(a_hi, a_lo, w_ref[...])


def _proj_residual(a, x, w, *, layer):
    m = x.shape[0]
    return pl.pallas_call(
        _proj_residual_kernel,
        grid=(D_MODEL // PROJ_TN,),
        in_specs=[
            pl.BlockSpec((m, D_MODEL), lambda j: (0, 0)),
            pl.BlockSpec((m, PROJ_TN), lambda j: (0, j)),
            pl.BlockSpec((None, D_MODEL, PROJ_TN), lambda j: (layer, 0, j)),
        ],
        out_specs=pl.BlockSpec((m, PROJ_TN), lambda j: (0, j)),
        out_shape=jax.ShapeDtypeStruct((m, D_MODEL), F32),
        compiler_params=pltpu.CompilerParams(
            dimension_semantics=("parallel",), vmem_limit_bytes=VMEM_LIMIT),
        name="proj_residual_split" if w.dtype == F32 else "proj_residual",
    )(a, x, w)


FFN_TF = 512
FFN_TM = 512
FFN_TF_SAMPLE = 1024


def _ffn_kernel(x_ref, g_ref, wr_ref, br_ref, wg_ref, wu_ref, wd_ref, o_ref, xhi_sc, xlo_sc, comb_sc, *, moe):
    e = pl.program_id(1)
    f = pl.program_id(2)
    precise = wg_ref.dtype == F32

    @pl.when((e == 0) & (f == 0))
    def _():
        x = x_ref[...]
        xn = _rms_norm(x, g_ref[...])
        o_ref[...] = x
        if precise or moe:
            xn_hi, xn_lo = _split(xn)
            xhi_sc[...] = xn_hi
            if precise:
                xlo_sc[...] = xn_lo
        else:
            xhi_sc[...] = xn.astype(BF16)
        if moe:
            logits = _mm(xn_hi, xn_lo, wr_ref[...]) + br_ref[...]
            lane = lax.broadcasted_iota(jnp.int32, logits.shape, 1)
            v1 = jnp.max(logits, axis=-1, keepdims=True)
            i1 = jnp.min(jnp.where(logits == v1, lane, LANES), axis=-1, keepdims=True)
            rest = jnp.where(lane == i1, NEG, logits)
            v2 = jnp.max(rest, axis=-1, keepdims=True)
            i2 = jnp.min(jnp.where(rest == v2, lane, LANES), axis=-1, keepdims=True)
            e2 = jnp.exp(v2 - v1)
            g1 = 1.0 / (1.0 + e2)
            g2 = e2 / (1.0 + e2)
            comb_sc[...] = jnp.where(lane == i1, g1, 0.0) + jnp.where(lane == i2, g2, 0.0)

    xn_hi = xhi_sc[...]
    xn_lo = xlo_sc[...] if precise else None
    gate = _mm(xn_hi, xn_lo, wg_ref[0])
    up = _mm(xn_hi, xn_lo, wu_ref[0])
    h = gate * _sigmoid(gate) * up
    if moe:
        lane = lax.broadcasted_iota(jnp.int32, comb_sc.shape, 1)
        h = h * jnp.sum(jnp.where(lane == e, comb_sc[...], 0.0), axis=-1, keepdims=True)
    h_hi, h_lo = _split(h) if precise else (h.astype(BF16), None)
    o_ref[...] += _mm(h_hi, h_lo, wd_ref[0])


def _ffn(x, g, wr, br, wg, wu, wd, *, layer, moe, tm, tf=FFN_TF):
    m = x.shape[0]
    _, n_e, _, d_f = wg.shape
    n_f = d_f // tf
    lo_rows = tm if wg.dtype == F32 else 16
    return pl.pallas_call(
        functools.partial(_ffn_kernel, moe=moe),
        grid=(m // tm, n_e, n_f),
        in_specs=[
            pl.BlockSpec((tm, D_MODEL), lambda i, e, f: (i, 0)),
            pl.BlockSpec((1, D_MODEL), lambda i, e, f: (0, 0)),
            pl.BlockSpec((None, D_MODEL, LANES), lambda i, e, f: (layer, 0, 0)),
            pl.BlockSpec((None, 1, LANES), lambda i, e, f: (layer, 0, 0)),
            pl.BlockSpec((None, 1, D_MODEL, tf), lambda i, e, f: (layer, e, 0, f)),
            pl.BlockSpec((None, 1, D_MODEL, tf), lambda i, e, f: (layer, e, 0, f)),
            pl.BlockSpec((None, 1, tf, D_MODEL), lambda i, e, f: (layer, e, f, 0)),
        ],
        out_specs=pl.BlockSpec((tm, D_MODEL), lambda i, e, f: (i, 0)),
        out_shape=jax.ShapeDtypeStruct((m, D_MODEL), F32),
        scratch_shapes=[pltpu.VMEM((tm, D_MODEL), BF16), pltpu.VMEM((lo_rows, D_MODEL), BF16),
                        pltpu.VMEM((tm, LANES), F32)],
        compiler_params=pltpu.CompilerParams(
            dimension_semantics=("parallel", "arbitrary", "arbitrary"), vmem_limit_bytes=VMEM_LIMIT),
        name=("ffn_moe" if moe else "ffn_dense") + ("_split" if wg.dtype == F32 else ""),
    )(x, g, wr, br, wg, wu, wd)


MOE_TM = 512
MOE_TF = 1024
MOE_ROUTE_TM = 512
MOE_MOVE_TM = 256


def _route_kernel(x_ref, g_ref, wr_ref, br_ref, o_ref):
    xn_hi, xn_lo = _split(_rms_norm(x_ref[...], g_ref[...]))
    logits = _mm(xn_hi, xn_lo, wr_ref[...]) + br_ref[...]
    lane = lax.broadcasted_iota(jnp.int32, logits.shape, 1)
    v1 = jnp.max(logits, axis=-1, keepdims=True)
    i1 = jnp.min(jnp.where(logits == v1, lane, LANES), axis=-1, keepdims=True)
    rest = jnp.where(lane == i1, NEG, logits)
    v2 = jnp.max(rest, axis=-1, keepdims=True)
    i2 = jnp.min(jnp.where(rest == v2, lane, LANES), axis=-1, keepdims=True)
    e2 = jnp.exp(v2 - v1)
    g1 = 1.0 / (1.0 + e2)
    g2 = e2 / (1.0 + e2)
    o_ref[...] = jnp.where(lane == 0, i1.astype(F32),
                           jnp.where(lane == 1, i2.astype(F32),
                                     jnp.where(lane == 2, g1, jnp.where(lane == 3, g2, 0.0))))


def _route(x, g, wr, br, *, layer):
    m = x.shape[0]
    tm = _row_tile(m, MOE_ROUTE_TM)
    return pl.pallas_call(
        _route_kernel,
        grid=(m // tm,),
        in_specs=[
            pl.BlockSpec((tm, D_MODEL), lambda i: (i, 0)),
            pl.BlockSpec((1, D_MODEL), lambda i: (0, 0)),
            pl.BlockSpec((None, D_MODEL, LANES), lambda i: (layer, 0, 0)),
            pl.BlockSpec((None, 1, LANES), lambda i: (layer, 0, 0)),
        ],
        out_specs=pl.BlockSpec((tm, LANES), lambda i: (i, 0)),
        out_shape=jax.ShapeDtypeStruct((m, LANES), F32),
        compiler_params=pltpu.CompilerParams(dimension_semantics=("parallel",), vmem_limit_bytes=VMEM_LIMIT),
        name="moe_route",
    )(x, g, wr, br)


def _moe_plan(route, n_tiles):
    experts = jnp.arange(N_EXPERTS, dtype=jnp.int32)
    e_pair = route[:, :2].astype(jnp.int32).reshape(-1)
    onehot = (e_pair[:, None] == experts[None, :]).astype(jnp.int32)
    csum = jnp.cumsum(onehot, axis=0)
    rank = jnp.sum(csum * onehot, axis=1) - 1
    tiles_e = (csum[-1] + MOE_TM - 1) // MOE_TM
    tend = jnp.cumsum(tiles_e)
    tstart = tend - tiles_e
    pos = (jnp.sum(onehot * tstart[None, :], axis=1) * MOE_TM + rank).astype(jnp.int32)
    n_used = tend[-1:].astype(jnp.int32)
    t_idx = jnp.arange(n_tiles, dtype=jnp.int32)
    tile_e = jnp.sum((t_idx[:, None] >= tend[None, :]).astype(jnp.int32), axis=1)
    last_e = jnp.max(jnp.where(tiles_e > 0, experts, 0))
    tile_e = jnp.minimum(tile_e, last_e).astype(jnp.int32)
    return pos, tile_e, n_used


def _row_copy(src, dst, sem):
    return pltpu.make_async_copy(src, dst, sem)


def _push_kernel(pos_ref, x_ref, g_ref, zeros_ref, xs_ref, xn_sc, sem):
    del zeros_ref
    tm = x_ref.shape[0]
    base = pl.program_id(0) * tm
    xn_sc[...] = _rms_norm(x_ref[...], g_ref[...])

    def start(r, carry):
        for c in range(2):
            dst = pos_ref[2 * (base + r) + c]
            _row_copy(xn_sc.at[pl.ds(r, 1)], xs_ref.at[pl.ds(dst, 1)], sem).start(priority=c)
        return carry

    lax.fori_loop(0, tm, start, 0, unroll=8)

    for c in range(2):
        _row_copy(xn_sc, xs_ref.at[pl.ds(0, tm)], sem).wait()


def _push(x, g, pos, n_rows):
    m = x.shape[0]
    tm = _row_tile(m, MOE_MOVE_TM)
    zeros = jnp.zeros((n_rows, D_MODEL), F32)
    return pl.pallas_call(
        _push_kernel,
        grid_spec=pltpu.PrefetchScalarGridSpec(
            num_scalar_prefetch=1,
            grid=(m // tm,),
            in_specs=[
                pl.BlockSpec((tm, D_MODEL), lambda i, pos: (i, 0)),
                pl.BlockSpec((1, D_MODEL), lambda i, pos: (0, 0)),
                pl.BlockSpec(memory_space=pl.ANY),
            ],
            out_specs=pl.BlockSpec(memory_space=pl.ANY),
            scratch_shapes=[pltpu.VMEM((tm, D_MODEL), F32), pltpu.SemaphoreType.DMA],
        ),
        out_shape=jax.ShapeDtypeStruct((n_rows, D_MODEL), F32),
        input_output_aliases={3: 0},
        compiler_params=pltpu.CompilerParams(dimension_semantics=("arbitrary",), vmem_limit_bytes=VMEM_LIMIT),
        name="moe_push",
    )(pos, x, g, zeros)


def _experts_kernel(tile_e_ref, n_used_ref, x_ref, wg_ref, wu_ref, wd_ref, o_ref, xb_sc):
    t = pl.program_id(0)
    f = pl.program_id(1)
    used = t < n_used_ref[0]

    @pl.when(f == 0)
    def _():
        xb_sc[...] = x_ref[...].astype(BF16)
        o_ref[...] = jnp.zeros_like(o_ref)

    @pl.when(used)
    def _():
        xn = xb_sc[...]
        gate = _dot(xn, wg_ref[0])
        up = _dot(xn, wu_ref[0])
        h = gate * _sigmoid(gate) * up
        o_ref[...] += _dot(h.astype(BF16), wd_ref[0])


def _experts(xs, tile_e, n_used, wg, wu, wd, *, layer):
    n_rows = xs.shape[0]
    n_f = wg.shape[3] // MOE_TF
    return pl.pallas_call(
        _experts_kernel,
        grid_spec=pltpu.PrefetchScalarGridSpec(
            num_scalar_prefetch=2,
            grid=(n_rows // MOE_TM, n_f),
            in_specs=[
                pl.BlockSpec((MOE_TM, D_MODEL), lambda t, f, te, nu: (t, 0)),
                pl.BlockSpec((None, 1, D_MODEL, MOE_TF), lambda t, f, te, nu: (layer, te[t], 0, f)),
                pl.BlockSpec((None, 1, D_MODEL, MOE_TF), lambda t, f, te, nu: (layer, te[t], 0, f)),
                pl.BlockSpec((None, 1, MOE_TF, D_MODEL), lambda t, f, te, nu: (layer, te[t], f, 0)),
            ],
            out_specs=pl.BlockSpec((MOE_TM, D_MODEL), lambda t, f, te, nu: (t, 0)),
            scratch_shapes=[pltpu.VMEM((MOE_TM, D_MODEL), BF16)],
        ),
        out_shape=jax.ShapeDtypeStruct((n_rows, D_MODEL), F32),
        compiler_params=pltpu.CompilerParams(
            dimension_semantics=("arbitrary", "arbitrary"), vmem_limit_bytes=VMEM_LIMIT),
        name="moe_experts",
    )(tile_e, n_used, xs, wg, wu, wd)


def _combine_kernel(pos_ref, x_ref, route_ref, ys_ref, o_ref, y_sc, sems):
    tm = x_ref.shape[0]
    i = pl.program_id(0)

    def gather(tile, slot):
        base = tile * tm

        def start(r, carry):
            for c in range(2):
                src = pos_ref[2 * (base + r) + c]
                _row_copy(ys_ref.at[pl.ds(src, 1)], y_sc.at[slot, c, pl.ds(r, 1)], sems.at[slot]).start(priority=c)
            return carry

        lax.fori_loop(0, tm, start, 0, unroll=8)

    @pl.when(i == 0)
    def _():
        gather(0, 0)

    @pl.when(i + 1 < pl.num_programs(0))
    def _():
        gather(i + 1, (i + 1) % 2)

    slot = i % 2

    for c in range(2):
        _row_copy(ys_ref.at[pl.ds(0, tm)], y_sc.at[slot, c], sems.at[slot]).wait()
    o_ref[...] = x_ref[...] + route_ref[:, 2:3] * y_sc[slot, 0] + route_ref[:, 3:4] * y_sc[slot, 1]


def _combine(x, route, pos, ys):
    m = x.shape[0]
    tm = _row_tile(m, MOE_MOVE_TM)
    return pl.pallas_call(
        _combine_kernel,
        grid_spec=pltpu.PrefetchScalarGridSpec(
            num_scalar_prefetch=1,
            grid=(m // tm,),
            in_specs=[
                pl.BlockSpec((tm, D_MODEL), lambda i, pos: (i, 0)),
                pl.BlockSpec((tm, LANES), lambda i, pos: (i, 0)),
                pl.BlockSpec(memory_space=pl.ANY),
            ],
            out_specs=pl.BlockSpec((tm, D_MODEL), lambda i, pos: (i, 0)),
            scratch_shapes=[pltpu.VMEM((2, 2, tm, D_MODEL), F32), pltpu.SemaphoreType.DMA((2,))],
        ),
        out_shape=jax.ShapeDtypeStruct((m, D_MODEL), F32),
        compiler_params=pltpu.CompilerParams(dimension_semantics=("arbitrary",), vmem_limit_bytes=VMEM_LIMIT),
        name="moe_combine",
    )(pos, x, route, ys)


def _moe_routed(x, g, wr, br, wg, wu, wd, *, layer):
    m = x.shape[0]
    n_tiles = (2 * m) // MOE_TM + N_EXPERTS
    route = _route(x, g, wr, br, layer=layer)
    pos, tile_e, n_used = _moe_plan(route, n_tiles)
    xs = _push(x, g, pos, n_tiles * MOE_TM)
    ys = _experts(xs, tile_e, n_used, wg, wu, wd, layer=layer)
    return _combine(x, route, pos, ys)


def _final_norm_kernel(x_ref, g_ref, o_ref):
    o_ref[...] = _rms_norm(x_ref[...], g_ref[...])


def _final_norm(x, g, *, tm):
    m = x.shape[0]
    return pl.pallas_call(
        _final_norm_kernel,
        grid=(m // tm,),
        in_specs=[pl.BlockSpec((tm, D_MODEL), lambda i: (i, 0)), pl.BlockSpec((1, D_MODEL), lambda i: (0, 0))],
        out_specs=pl.BlockSpec((tm, D_MODEL), lambda i: (i, 0)),
        out_shape=jax.ShapeDtypeStruct((m, D_MODEL), F32),
        compiler_params=pltpu.CompilerParams(dimension_semantics=("parallel",), vmem_limit_bytes=VMEM_LIMIT),
        name="final_norm",
    )(x, g)


def _rope_tables(pos):
    half = HEAD_DIM // 2
    inv = ROPE_THETA ** (-2.0 * jnp.arange(half, dtype=F32) / HEAD_DIM)
    ang = pos.astype(F32)[:, None] * inv[None, :]
    cos, sin = jnp.cos(ang), jnp.sin(ang)
    reps = LANES // HEAD_DIM
    return (jnp.tile(jnp.concatenate([cos, cos], axis=-1), (1, reps)),
            jnp.tile(jnp.concatenate([-sin, sin], axis=-1), (1, reps)))


def _row_tile(m, cap):
    t = cap
    while m % t:
        t //= 2
    return t


def _put_tail(main, tail, batch, seq):
    c = main.shape[1]
    upd = tail.reshape(batch, BLOCK, c).astype(main.dtype)
    return lax.dynamic_update_slice(main.reshape(batch, seq, c), upd, (0, seq - BLOCK, 0)).reshape(batch * seq, c)


def kernel(x_prompt, x_sample, cache_k, cache_v, state_pool, norm_mix, norm_ffn, norm_final, w_in, attn_sinks,
           w_pool_map, pool_scale, w_out, w_ffn_gate, w_ffn_up, w_ffn_down, w_router, b_router, w_exp_gate,
           w_exp_up, w_exp_down):
    batch, seq, _ = x_prompt.shape
    dec_batch, dec_seq, _ = x_sample.shape
    depth = w_in.shape[0]
    assert seq % BLOCK == 0 and dec_batch % SAMPLE_DB == 0 and dec_seq <= POOL_HIST

    w_in_b = w_in.astype(BF16)
    w_pool_b = w_pool_map.astype(BF16)
    w_out_b = w_out.astype(BF16)
    w_fg, w_fu, w_fd = (w[:, None] for w in (w_ffn_gate, w_ffn_up, w_ffn_down))
    w_fg_b, w_fu_b, w_fd_b = (w.astype(BF16) for w in (w_fg, w_fu, w_fd))
    w_eg_b, w_eu_b, w_ed_b = (w.astype(BF16) for w in (w_exp_gate, w_exp_up, w_exp_down))
    w_r = jnp.pad(w_router.astype(F32), ((0, 0), (0, 0), (0, LANES - N_EXPERTS)))
    b_r = jnp.pad(b_router.astype(F32), ((0, 0), (0, LANES - N_EXPERTS)), constant_values=NEG)[:, None]
    zero_wr = jnp.zeros((w_fg.shape[0], D_MODEL, LANES), F32)
    zero_br = jnp.zeros((w_fg.shape[0], 1, LANES), F32)

    pos_p = jnp.arange(seq, dtype=jnp.int32)
    pos_s = jnp.tile(PAST_LEN + jnp.arange(dec_seq, dtype=jnp.int32), dec_batch)
    cos_p, sin_p = _rope_tables(pos_p)
    cos_s, sin_s = _rope_tables(pos_s)
    cos_x, sin_x = _rope_tables(jnp.concatenate([jnp.tile(pos_p[seq - BLOCK:], batch), pos_s]))

    m_p, m_s, m_t = batch * seq, dec_batch * dec_seq, batch * BLOCK
    tm_in_p = _row_tile(seq, IN_TM)
    tm_ffn_p = _row_tile(m_p, FFN_TM)

    xp = x_prompt.reshape(m_p, D_MODEL)
    xs = x_sample.reshape(m_s, D_MODEL)
    ck = cache_k.reshape(depth, dec_batch, WINDOW, KV_DIM)
    cv = cache_v.reshape(depth, dec_batch, WINDOW, KV_DIM)

    n_split = min(2, depth)
    xx = jnp.concatenate([x_prompt[:, seq - BLOCK:].reshape(m_t, D_MODEL), xs], axis=0)

    kp_l, vp_l, pp_l, ks_l, vs_l, ps_l = [], [], [], [], [], []
    for l in range(depth):
        g_mix = norm_mix[l][None]
        g_ffn = norm_ffn[l][None]
        pscale = pool_scale[l][None]
        sinks = attn_sinks[l]
        split = l < n_split

        q, k, v, u, ga, gp = _inproj(xp, g_mix, w_in_b, cos_p, sin_p, layer=l, tm=tm_in_p)
        if split:
            qx, kx, vx, ux, gax, gpx = _inproj(xx, g_mix, w_in, cos_x, sin_x, layer=l, tm=m_t + m_s)
            k, v, u = (_put_tail(a, b[:m_t], batch, seq) for a, b in ((k, kx), (v, vx), (u, ux)))
        xp = _mixer_prompt(q, k, v, u, ga, gp, xp, sinks, w_pool_b, pscale, w_out_b, layer=l, batch=batch, seq=seq)
        if split:
            mix_t = _mixer_prompt_tail(qx[:m_t], k, v, u, gax[:m_t], gpx[:m_t], sinks, w_pool_map, pscale,
                                       layer=l, batch=batch, seq=seq)
            mix_s, nk, nv, npool = _mixer_sample(qx[m_t:], kx[m_t:], vx[m_t:], ux[m_t:], gax[m_t:], gpx[m_t:],
                                                 ck, cv, state_pool, sinks, w_pool_map, pscale,
                                                 layer=l, dec_seq=dec_seq)
            xx = _proj_residual(jnp.concatenate([mix_t, mix_s], axis=0), xx, w_out, layer=l)
            xp = _put_tail(xp, xx[:m_t], batch, seq)
        else:
            qs, ks, vs, us, gas, gps = _inproj(xs, g_mix, w_in_b, cos_s, sin_s, layer=l, tm=m_s)
            mix_s, nk, nv, npool = _mixer_sample(qs, ks, vs, us, gas, gps, ck, cv, state_pool, sinks,
                                                 w_pool_b, pscale, layer=l, dec_seq=dec_seq)
            xs = _proj_residual(mix_s, xs, w_out_b, layer=l)
        kp_l.append(k.reshape(batch, seq, N_KV_HEADS, HEAD_DIM)[:, seq - WINDOW:])
        vp_l.append(v.reshape(batch, seq, N_KV_HEADS, HEAD_DIM)[:, seq - WINDOW:])
        pp_l.append(u.reshape(batch, seq, POOL_DIM)[:, seq - POOL_HIST:])
        ks_l.append(nk.reshape(dec_batch, WINDOW, N_KV_HEADS, HEAD_DIM))
        vs_l.append(nv.reshape(dec_batch, WINDOW, N_KV_HEADS, HEAD_DIM))
        ps_l.append(npool)

        i = l // 2
        if l % 2 == 0:
            xp = _ffn(xp, g_ffn, zero_wr, zero_br, w_fg_b, w_fu_b, w_fd_b, layer=i, moe=False, tm=tm_ffn_p)
            if split and l + 1 < n_split:
                xx = _ffn(xx, g_ffn, zero_wr, zero_br, w_fg, w_fu, w_fd, layer=i, moe=False, tm=m_t + m_s)
                xp = _put_tail(xp, xx[:m_t], batch, seq)
            else:
                if split:
                    xs = xx[m_t:]
                xs = _ffn(xs, g_ffn, zero_wr, zero_br, w_fg_b, w_fu_b, w_fd_b, layer=i, moe=False, tm=m_s)
        else:
            if split:
                xs = xx[m_t:]
            xp = _moe_routed(xp, g_ffn, w_r, b_r, w_eg_b, w_eu_b, w_ed_b, layer=i)
            xs = _ffn(xs, g_ffn, w_r, b_r, w_eg_b, w_eu_b, w_ed_b, layer=i, moe=True, tm=m_s, tf=FFN_TF_SAMPLE)

    g_fin = norm_final[None]
    y_p = _final_norm(xp, g_fin, tm=tm_ffn_p).reshape(batch, seq, D_MODEL)
    y_s = _final_norm(xs, g_fin, tm=m_s).reshape(dec_batch, dec_seq, D_MODEL)
    return (y_p, y_s, jnp.stack(kp_l), jnp.stack(vp_l), jnp.stack(pp_l),
            jnp.stack(ks_l), jnp.stack(vs_l), jnp.stack(ps_l))
```
